```python
import jax, jax.numpy as jnp
from jax import lax
import numpy as np

D_MODEL = 4096
BATCH = 1
SEQ = 8192
DEPTH = 2
DEC_BATCH = 8
DEC_SEQ = 32
PAST_LEN = 2048

CHUNK = 64
N_META = 16
NORM_EPS = 1e-6
RW_HEADS = 24
RW_HD = 64
RW_W = RW_HEADS * RW_HD
RW_R_W = 64
RW_R_A = 64
RW_R_G = 128
RW_P = 3 * RW_W + RW_R_W + RW_R_A + RW_R_G
RW_GN_EPS = 64e-5
GD_HEADS = 12
GD_HD = 128
GD_W = GD_HEADS * GD_HD
GD_CONV = 4
GD_P = 4 * GD_W + 2 * GD_HEADS
ML_HEADS = 4
ML_DK = 128
ML_DV = 256
ML_W = ML_HEADS * ML_DV
ML_P = 2 * ML_HEADS * ML_DK + 2 * ML_W + 2 * ML_HEADS
MIX_W = RW_W + GD_W + ML_W
P_IN = RW_P + GD_P + ML_P
D_FF = 11008
FFN_CONV = 3

kernel_name = 'hybrid_stream_rwkv7_gdn_mlstm'


def rms_norm(x, w):
    xf = x.astype(jnp.float32)
    y = xf * lax.rsqrt(jnp.mean(xf * xf, -1, keepdims=True) + NORM_EPS)
    return (y * w.astype(jnp.float32)).astype(x.dtype)


def head_rms(x, w):
    return x * lax.rsqrt(jnp.mean(x * x, -1, keepdims=True) + NORM_EPS) * w.astype(jnp.float32)


def l2norm(x):
    return x * lax.rsqrt(jnp.sum(x * x, -1, keepdims=True) + 1e-6)


def causal_dwconv(x, buf, w):
    K = w.shape[0]
    L = x.shape[1]
    xp = jnp.concatenate([buf.astype(x.dtype), x], axis=1)
    out = sum(xp[:, j:j + L] * w[j] for j in range(K))
    return out, xp[:, L:]


def scan_blocks(step, seqs, state, segments):
    outs, start = [], 0
    for length, block in segments:
        n = length // block
        xs = tuple(jnp.moveaxis(s[:, :, start:start + length].reshape(s.shape[:2] + (n, block) + s.shape[3:]), 2, 0) for s in seqs)
        state, o = lax.scan(step, state, xs)
        o = jnp.moveaxis(o, 0, 2)
        outs.append(o.reshape(o.shape[:2] + (length,) + o.shape[4:]))
        start += length
    return jnp.concatenate(outs, axis=2), state


def gdn_block(S, blk):
    q, k, v, g, beta = blk
    C = q.shape[2]
    incl = jnp.tril(jnp.ones((C, C), bool))
    strict = jnp.tril(jnp.ones((C, C), bool), -1)
    G = jnp.cumsum(g, -1)
    decay = jnp.where(incl, jnp.exp(jnp.where(incl, G[..., :, None] - G[..., None, :], 0.0)), 0.0)
    kk = jnp.einsum('bhik,bhjk->bhij', k, k)
    A = jnp.eye(C, dtype=q.dtype) + jnp.where(strict, beta[..., :, None] * kk * decay, 0.0)
    rhs = jnp.concatenate([v * beta[..., None], k * (beta * jnp.exp(G))[..., None]], -1)
    sol = lax.linalg.triangular_solve(A, rhs, left_side=True, lower=True, unit_diagonal=True)
    u0, wk = sol[..., :GD_HD], sol[..., GD_HD:]
    u = u0 - jnp.einsum('bhik,bhvk->bhiv', wk, S)
    qk = jnp.einsum('bhik,bhjk->bhij', q, k) * decay
    o = jnp.exp(G)[..., None] * jnp.einsum('bhik,bhvk->bhiv', q, S) + jnp.einsum('bhij,bhjv->bhiv', qk, u)
    gl = G[..., -1]
    S = jnp.exp(gl)[..., None, None] * S + jnp.einsum('bhjv,bhjk->bhvk', u * jnp.exp(gl[..., None] - G)[..., None], k)
    return S, o


def mlstm_block(state, blk):
    Cs, ns, m0 = state
    q, k, v, li, lf = blk
    C = q.shape[2]
    incl = jnp.tril(jnp.ones((C, C), bool))
    F = jnp.cumsum(lf, -1)
    D = jnp.where(incl, F[..., :, None] - F[..., None, :] + li[..., None, :], -jnp.inf)
    inter = F + m0[..., None]
    m = jnp.maximum(inter, jnp.max(D, -1))
    Dw = jnp.exp(D - m[..., None])
    wi = jnp.exp(inter - m)
    s = jnp.einsum('bhik,bhjk->bhij', q, k) * Dw
    num = wi[..., None] * jnp.einsum('bhik,bhvk->bhiv', q, Cs) + jnp.einsum('bhij,bhjv->bhiv', s, v)
    den = wi * jnp.einsum('bhik,bhk->bhi', q, ns) + jnp.sum(s, -1)
    h = num / jnp.maximum(jnp.abs(den), jnp.exp(-m))[..., None]
    mC = m[..., -1]
    wend = jnp.exp(F[..., -1:] - F + li - mC[..., None])
    dstate = jnp.exp(F[..., -1] + m0 - mC)
    Cs = dstate[..., None, None] * Cs + jnp.einsum('bhjv,bhjk->bhvk', v * wend[..., None], k)
    ns = dstate[..., None] * ns + jnp.einsum('bhj,bhjk->bhk', wend, k)
    return (Cs, ns, mC), h


def rwkv_mix(p, shift_buf, S0, lp):
    B, L, _ = p.shape
    prev = jnp.concatenate([shift_buf[:, None].astype(p.dtype), p[:, :-1]], 1)
    xs = p + (prev - p) * lp['rwkv_mu']
    r, k, v, dw, da, dg = jnp.split(xs, [RW_W, 2 * RW_W, 3 * RW_W, 3 * RW_W + RW_R_W, 3 * RW_W + RW_R_W + RW_R_A], axis=-1)
    log_w = -jnp.exp(-jax.nn.softplus(-(lp['rwkv_w0'] + jnp.tanh(dw) @ lp['rwkv_w_up'])) - 0.5)
    a = jax.nn.sigmoid(lp['rwkv_a0'] + da @ lp['rwkv_a_up'])
    g = jax.nn.sigmoid(dg) @ lp['rwkv_g_up']
    heads = lambda t: t.reshape(B, L, RW_HEADS, RW_HD)
    kk = l2norm(heads(k * lp['rwkv_k_k']))
    k_t = k * (1.0 + (a - 1.0) * lp['rwkv_k_a'])
    r_h, w_h, k_h, v_h, a_h = heads(r), heads(jnp.exp(log_w)), heads(k_t), heads(v), heads(a)

    def step(S, inp):
        r_, w_, k_, v_, kk_, a_ = inp
        sa = jnp.einsum('bhvk,bhk->bhv', S, -kk_)
        S = S * w_[:, :, None, :] + sa[..., None] * (kk_ * a_)[:, :, None, :] + v_[..., None] * k_[:, :, None, :]
        return S, jnp.einsum('bhvk,bhk->bhv', S, r_)

    seq = tuple(jnp.moveaxis(t, 1, 0) for t in (r_h, w_h, k_h, v_h, kk, a_h))
    S, y = lax.scan(step, S0.astype(jnp.float32), seq)
    y = jnp.moveaxis(y, 0, 1)
    mu = jnp.mean(y, -1, keepdims=True)
    var = jnp.mean((y - mu) ** 2, -1, keepdims=True)
    y = ((y - mu) * lax.rsqrt(var + RW_GN_EPS)).reshape(B, L, RW_W) * lp['rwkv_ln_w'] + lp['rwkv_ln_b']
    bonus = (jnp.sum(r_h * k_h * lp['rwkv_r_k'].reshape(RW_HEADS, RW_HD), -1, keepdims=True) * v_h).reshape(B, L, RW_W)
    return (y + bonus) * g, p[:, -1], S


def gdn_mix(p, conv_buf, S0, lp, segments):
    B, L, _ = p.shape
    qkv, a_, b_, z = jnp.split(p, [3 * GD_W, 3 * GD_W + GD_HEADS, 3 * GD_W + 2 * GD_HEADS], axis=-1)
    qkv_c, new_buf = causal_dwconv(qkv, conv_buf, lp['gdn_conv_w'])
    q, k, v = jnp.split(jax.nn.silu(qkv_c), 3, axis=-1)
    to_bhld = lambda t: t.reshape(B, L, GD_HEADS, GD_HD).transpose(0, 2, 1, 3)
    q = l2norm(to_bhld(q)) * GD_HD ** -0.5
    k = l2norm(to_bhld(k))
    v = to_bhld(v)
    g = (-jnp.exp(lp['gdn_a_log']) * jax.nn.softplus(a_ + lp['gdn_dt_bias'])).transpose(0, 2, 1)
    beta = jax.nn.sigmoid(b_).transpose(0, 2, 1)
    o, S = scan_blocks(gdn_block, (q, k, v, g, beta), S0.astype(jnp.float32), segments)
    o = head_rms(o.transpose(0, 2, 1, 3), lp['gdn_norm_w']) * jax.nn.silu(z.reshape(B, L, GD_HEADS, GD_HD))
    return o.reshape(B, L, GD_W), new_buf, S


def mlstm_mix(p, C0, n0, m0, lp, segments):
    B, L, _ = p.shape
    qd = ML_HEADS * ML_DK
    q, k, v, i_, f_, o_ = jnp.split(p, [qd, 2 * qd, 2 * qd + ML_W, 2 * qd + ML_W + ML_HEADS, 2 * qd + ML_W + 2 * ML_HEADS], axis=-1)
    q = q.reshape(B, L, ML_HEADS, ML_DK).transpose(0, 2, 1, 3) * ML_DK ** -0.5
    k = k.reshape(B, L, ML_HEADS, ML_DK).transpose(0, 2, 1, 3)
    v = v.reshape(B, L, ML_HEADS, ML_DV).transpose(0, 2, 1, 3)
    li = (i_ + lp['mlstm_i_b']).transpose(0, 2, 1)
    lf = jax.nn.log_sigmoid(f_ + lp['mlstm_f_b']).transpose(0, 2, 1)
    init = (C0.astype(jnp.float32), n0.astype(jnp.float32), m0.astype(jnp.float32))
    h, (Cs, ns, m) = scan_blocks(mlstm_block, (q, k, v, li, lf), init, segments)
    h = head_rms(h.transpose(0, 2, 1, 3), lp['mlstm_norm_w'].reshape(ML_HEADS, ML_DV))
    h = h * jax.nn.sigmoid(o_).reshape(B, L, ML_HEADS, ML_DV)
    return h.reshape(B, L, ML_W), Cs, ns, m


def trunk_layer(x, states, lp, segments):
    rw_S, rw_shift, gd_S, gd_conv, ml_C, ml_n, ml_m, ffn_buf = states
    h = rms_norm(x, lp['norm_mix_w'])
    p = jnp.matmul(h, lp['w_in']).astype(jnp.float32)
    p_rw, p_gd, p_ml = jnp.split(p, [RW_P, RW_P + GD_P], axis=-1)
    o_rw, rw_shift, rw_S = rwkv_mix(p_rw, rw_shift, rw_S, lp)
    o_gd, gd_conv, gd_S = gdn_mix(p_gd, gd_conv, gd_S, lp, segments)
    o_ml, ml_C, ml_n, ml_m = mlstm_mix(p_ml, ml_C, ml_n, ml_m, lp, segments)
    mix = jnp.concatenate([o_rw, o_gd, o_ml], axis=-1).astype(x.dtype)
    x = x + jnp.matmul(mix, lp['w_out'])
    h = rms_norm(x, lp['norm_ffn_w'])
    gate, ffn_buf = causal_dwconv(jnp.matmul(h, lp['ffn_w_gate']), ffn_buf, lp['ffn_conv_w'])
    x = x + jnp.matmul(jax.nn.silu(gate + lp['ffn_conv_b']) * jnp.matmul(h, lp['ffn_w_up']), lp['ffn_w_down'])
    return x, (rw_S, rw_shift, gd_S, gd_conv, ml_C, ml_n, ml_m, ffn_buf)


def setup_inputs(seed: int = 0) -> dict:
    key = jax.random.key(seed)
    ks = iter(jax.random.split(key, 48))
    f32 = jnp.float32
    nrm = lambda shape, scale: jax.random.normal(next(ks), shape, f32) * scale
    uni = lambda shape, lo, hi: jax.random.uniform(next(ks), shape, f32, lo, hi)
    return {
        'x_prompt': nrm((BATCH, SEQ, D_MODEL), 1.0),
        'x_sample': nrm((DEC_BATCH, DEC_SEQ, D_MODEL), 1.0),
        'state_rwkv_wkv': nrm((DEPTH, DEC_BATCH, RW_HEADS, RW_HD, RW_HD), 0.3),
        'state_rwkv_shift': nrm((DEPTH, DEC_BATCH, RW_P), 1.0),
        'state_gdn': nrm((DEPTH, DEC_BATCH, GD_HEADS, GD_HD, GD_HD), 0.3),
        'cache_gdn_conv': nrm((DEPTH, DEC_BATCH, GD_CONV - 1, 3 * GD_W), 1.0),
        'state_mlstm_c': nrm((DEPTH, DEC_BATCH, ML_HEADS, ML_DV, ML_DK), 0.3),
        'state_mlstm_n': nrm((DEPTH, DEC_BATCH, ML_HEADS, ML_DK), 0.3),
        'state_mlstm_m': nrm((DEPTH, DEC_BATCH, ML_HEADS), 0.5),
        'cache_ffn_conv': nrm((DEPTH, DEC_BATCH, FFN_CONV - 1, D_FF), 1.0),
        'meta_tokens': nrm((N_META, D_MODEL), 1.0),
        'norm_mix_w': 1.0 + nrm((DEPTH, D_MODEL), 0.02),
        'w_in': nrm((DEPTH, D_MODEL, P_IN), D_MODEL ** -0.5),
        'rwkv_mu': uni((DEPTH, RW_P), 0.0, 1.0),
        'rwkv_w0': nrm((DEPTH, RW_W), 0.5),
        'rwkv_w_up': nrm((DEPTH, RW_R_W, RW_W), 0.5 * RW_R_W ** -0.5),
        'rwkv_a0': nrm((DEPTH, RW_W), 0.3),
        'rwkv_a_up': nrm((DEPTH, RW_R_A, RW_W), 0.5 * RW_R_A ** -0.5),
        'rwkv_g_up': nrm((DEPTH, RW_R_G, RW_W), RW_R_G ** -0.5),
        'rwkv_k_k': 0.85 + nrm((DEPTH, RW_W), 0.05),
        'rwkv_k_a': 1.0 + nrm((DEPTH, RW_W), 0.05),
        'rwkv_r_k': nrm((DEPTH, RW_W), 0.1),
        'rwkv_ln_w': 1.0 + nrm((DEPTH, RW_W), 0.02),
        'rwkv_ln_b': nrm((DEPTH, RW_W), 0.02),
        'gdn_conv_w': nrm((DEPTH, GD_CONV, 3 * GD_W), GD_CONV ** -0.5),
        'gdn_a_log': jnp.log(uni((DEPTH, GD_HEADS), 1.0, 16.0)),
        'gdn_dt_bias': jnp.log(jnp.expm1(uni((DEPTH, GD_HEADS), 0.001, 0.1))),
        'gdn_norm_w': 1.0 + nrm((DEPTH, GD_HD), 0.02),
        'mlstm_i_b': nrm((DEPTH, ML_HEADS), 0.5),
        'mlstm_f_b': 3.0 + nrm((DEPTH, ML_HEADS), 0.5),
        'mlstm_norm_w': 1.0 + nrm((DEPTH, ML_W), 0.02),
        'w_out': nrm((DEPTH, MIX_W, D_MODEL), MIX_W ** -0.5),
        'norm_ffn_w': 1.0 + nrm((DEPTH, D_MODEL), 0.02),
        'ffn_w_gate': nrm((DEPTH, D_MODEL, D_FF), D_MODEL ** -0.5),
        'ffn_w_up': nrm((DEPTH, D_MODEL, D_FF), D_MODEL ** -0.5),
        'ffn_conv_w': nrm((DEPTH, FFN_CONV, D_FF), FFN_CONV ** -0.5),
        'ffn_conv_b': nrm((DEPTH, D_FF), 0.02),
        'ffn_w_down': nrm((DEPTH, D_FF, D_MODEL), D_FF ** -0.5),
        'final_norm_w': 1.0 + nrm((D_MODEL,), 0.02),
    }


def reference(x_prompt, x_sample, state_rwkv_wkv, state_rwkv_shift, state_gdn, cache_gdn_conv,
              state_mlstm_c, state_mlstm_n, state_mlstm_m, cache_ffn_conv, meta_tokens, norm_mix_w, w_in,
              rwkv_mu, rwkv_w0, rwkv_w_up, rwkv_a0, rwkv_a_up, rwkv_g_up, rwkv_k_k, rwkv_k_a, rwkv_r_k,
              rwkv_ln_w, rwkv_ln_b, gdn_conv_w, gdn_a_log, gdn_dt_bias, gdn_norm_w, mlstm_i_b, mlstm_f_b,
              mlstm_norm_w, w_out, norm_ffn_w, ffn_w_gate, ffn_w_up, ffn_conv_w, ffn_conv_b, ffn_w_down,
              final_norm_w):
    f32 = jnp.float32
    dec_seq = x_sample.shape[1]
    prompt_segments = ((N_META, N_META), (x_prompt.shape[1], CHUNK))
    sample_segments = ((dec_seq, dec_seq),)
    xp = jnp.concatenate([jnp.broadcast_to(meta_tokens.astype(x_prompt.dtype)[None], (BATCH, N_META, D_MODEL)), x_prompt], axis=1)
    xs = x_sample
    zero_states = (jnp.zeros((BATCH, RW_HEADS, RW_HD, RW_HD), f32), jnp.zeros((BATCH, RW_P), f32),
                   jnp.zeros((BATCH, GD_HEADS, GD_HD, GD_HD), f32), jnp.zeros((BATCH, GD_CONV - 1, 3 * GD_W), f32),
                   jnp.zeros((BATCH, ML_HEADS, ML_DV, ML_DK), f32), jnp.zeros((BATCH, ML_HEADS, ML_DK), f32),
                   jnp.zeros((BATCH, ML_HEADS), f32), jnp.zeros((BATCH, FFN_CONV - 1, D_FF), x_prompt.dtype))
    p_states, s_states = [], []
    for l in range(DEPTH):
        lp = {'norm_mix_w': norm_mix_w[l], 'w_in': w_in[l], 'rwkv_mu': rwkv_mu[l], 'rwkv_w0': rwkv_w0[l],
              'rwkv_w_up': rwkv_w_up[l], 'rwkv_a0': rwkv_a0[l], 'rwkv_a_up': rwkv_a_up[l], 'rwkv_g_up': rwkv_g_up[l],
              'rwkv_k_k': rwkv_k_k[l], 'rwkv_k_a': rwkv_k_a[l], 'rwkv_r_k': rwkv_r_k[l], 'rwkv_ln_w': rwkv_ln_w[l],
              'rwkv_ln_b': rwkv_ln_b[l], 'gdn_conv_w': gdn_conv_w[l], 'gdn_a_log': gdn_a_log[l],
              'gdn_dt_bias': gdn_dt_bias[l], 'gdn_norm_w': gdn_norm_w[l], 'mlstm_i_b': mlstm_i_b[l],
              'mlstm_f_b': mlstm_f_b[l], 'mlstm_norm_w': mlstm_norm_w[l], 'w_out': w_out[l],
              'norm_ffn_w': norm_ffn_w[l], 'ffn_w_gate': ffn_w_gate[l], 'ffn_w_up': ffn_w_up[l],
              'ffn_conv_w': ffn_conv_w[l], 'ffn_conv_b': ffn_conv_b[l], 'ffn_w_down': ffn_w_down[l]}
        xp, st_p = trunk_layer(xp, zero_states, lp, prompt_segments)
        st_in = (state_rwkv_wkv[l], state_rwkv_shift[l], state_gdn[l], cache_gdn_conv[l],
                 state_mlstm_c[l], state_mlstm_n[l], state_mlstm_m[l], cache_ffn_conv[l])
        xs, st_s = trunk_layer(xs, st_in, lp, sample_segments)
        p_states.append(st_p)
        s_states.append(st_s)
    y_prompt = rms_norm(xp, final_norm_w)[:, N_META:]
    y_sample = rms_norm(xs, final_norm_w)
    p_rwkv_wkv, p_rwkv_shift, p_gdn, p_gdn_conv, p_mlstm_c, p_mlstm_n, p_mlstm_m, p_ffn_conv = [jnp.stack([st[i] for st in p_states], 0) for i in range(8)]
    s_rwkv_wkv, s_rwkv_shift, s_gdn, s_gdn_conv, s_mlstm_c, s_mlstm_n, s_mlstm_m, s_ffn_conv = [jnp.stack([st[i] for st in s_states], 0) for i in range(8)]
    return (y_prompt, y_sample, p_rwkv_wkv, p_rwkv_shift, p_gdn, p_gdn_conv, p_mlstm_c, p_mlstm_n, p_mlstm_m, p_ffn_conv,
            s_rwkv_wkv, s_rwkv_shift, s_gdn, s_gdn_conv, s_mlstm_c, s_mlstm_n, s_mlstm_m, s_ffn_conv)
```

```python
import functools
import math

import jax
import jax.numpy as jnp
from jax import lax
from jax.experimental import pallas as pl
from jax.experimental.pallas import tpu as pltpu

F32 = jnp.float32
BF16 = jnp.bfloat16

D_MODEL = 4096
N_META = 16
CHUNK = 64
NORM_EPS = 1e-6
RW_HEADS, RW_HD = 24, 64
RW_W = RW_HEADS * RW_HD
RW_P = 3 * RW_W + 64 + 64 + 128
RW_GN_EPS = 64e-5
GD_HEADS, GD_HD = 12, 128
GD_W = GD_HEADS * GD_HD
GD_P = 4 * GD_W + 2 * GD_HEADS
ML_HEADS, ML_DK, ML_DV = 4, 128, 256
ML_W = ML_HEADS * ML_DV
ML_QK = ML_HEADS * ML_DK
D_FF = 11008

LANE = 128
TAIL = 8
ROW_META = 48
ROW_PROMPT = 64
ROW_TILE = 512
COL_TILE = 1024
P_PAD = 14 * COL_TILE
FF_PAD = 11 * COL_TILE
FF_KTILE = FF_PAD // 4
VMEM_LIMIT = 56 * 1024 * 1024

COL_GD_QKV = 0
COL_GD_AB = 4608
COL_ML_IF = 4736
COL_RW = 4864
COL_ML_Q = 9728
COL_ML_K = 10240
COL_GD_Z = 10752
COL_ML_V = 12288
COL_ML_O = 13312


def _cparams(n_axes):
    return pltpu.CompilerParams(dimension_semantics=("arbitrary",) * n_axes, vmem_limit_bytes=VMEM_LIMIT)


def _bdot(a, b):
    return jnp.dot(a.astype(BF16), b.astype(BF16), preferred_element_type=F32)


def _bdot_nt(a, b):
    return lax.dot_general(a.astype(BF16), b.astype(BF16), (((1,), (1,)), ((), ())), preferred_element_type=F32)


def _bdot_tn(a, b):
    return lax.dot_general(a.astype(BF16), b.astype(BF16), (((0,), (0,)), ((), ())), preferred_element_type=F32)


def _split3(x):
    hi = x.astype(BF16)
    r1 = x - hi.astype(F32)
    mid = r1.astype(BF16)
    lo = (r1 - mid.astype(F32)).astype(BF16)
    return hi, mid, lo


def _cumsum_rows(x, tril_bf):
    hi, mid, lo = _split3(x)
    d = lambda t: jnp.dot(tril_bf, t, preferred_element_type=F32)
    return d(hi) + d(mid) + d(lo)


def _cols_to_rows(x, n_rows):
    sel = (lax.broadcasted_iota(jnp.int32, (n_rows, LANE), 0) == lax.broadcasted_iota(jnp.int32, (n_rows, LANE), 1)).astype(BF16)
    hi, mid, lo = _split3(x)
    d = lambda t: lax.dot_general(sel, t, (((1,), (1,)), ((), ())), preferred_element_type=F32)
    return d(hi) + d(mid) + d(lo)


def _tri_masks(C):
    ri = lax.broadcasted_iota(jnp.int32, (C, C), 0)
    ci = lax.broadcasted_iota(jnp.int32, (C, C), 1)
    return ri >= ci, ri > ci, ri == ci


def _unit_lower_inverse(A, eye, C):
    N = -A
    T = jnp.where(eye, 1.0, N)
    P = N
    for _ in range(int(math.log2(C)) - 1):
        P = _bdot(P, P)
        T = T + _bdot(T, P)
    return T


def _sigmoid(x):
    return jax.nn.sigmoid(x)


def _softplus(x):
    return jnp.maximum(x, 0.0) + jnp.log(1.0 + jnp.exp(-jnp.abs(x)))


def _shift_rows(x, tail, s, C):
    xr = pltpu.roll(x, s, 0)
    r8 = lax.broadcasted_iota(jnp.int32, (TAIL, 1), 0)
    top = jnp.where(r8 < s, pltpu.roll(tail, s, 0), xr[0:TAIL])
    return jnp.concatenate([top, xr[TAIL:]], axis=0)


def _rmsnorm_body(x_ref, w_ref, o_ref):
    x = x_ref[...]
    y = x * lax.rsqrt(jnp.mean(x * x, -1, keepdims=True) + NORM_EPS)
    o_ref[...] = (y * w_ref[...]).astype(o_ref.dtype)


def _rmsnorm(x, w, out_dtype, rows, row_base, block):
    d = x.shape[1]
    rb = row_base // block
    return pl.pallas_call(
        _rmsnorm_body,
        grid=(rows // block,),
        in_specs=[pl.BlockSpec((block, d), lambda i: (rb + i, 0)), pl.BlockSpec((1, d), lambda i: (0, 0))],
        out_specs=pl.BlockSpec((block, d), lambda i: (i, 0)),
        out_shape=jax.ShapeDtypeStruct((rows, d), out_dtype),
        compiler_params=_cparams(1),
        name="rmsnorm",
    )(x, w.reshape(1, d))


def _mm_body(a_ref, b_ref, o_ref):
    o_ref[...] = jnp.dot(a_ref[...], b_ref[...], preferred_element_type=F32)


def _mm_res_body(a_ref, b_ref, r_ref, o_ref):
    o_ref[...] = r_ref[...] + jnp.dot(a_ref[...], b_ref[...], preferred_element_type=F32)


def _matmul(a, b, res=None, name="matmul"):
    m, k = a.shape
    n = b.shape[1]
    grid = (n // COL_TILE, m // ROW_TILE)
    a_spec = pl.BlockSpec((ROW_TILE, k), lambda j, i: (i, 0))
    b_spec = pl.BlockSpec((k, COL_TILE), lambda j, i: (0, j))
    o_spec = pl.BlockSpec((ROW_TILE, COL_TILE), lambda j, i: (i, j))
    if res is None:
        body, specs, args = _mm_body, [a_spec, b_spec], (a, b)
    else:
        body, specs, args = _mm_res_body, [a_spec, b_spec, o_spec], (a, b, res)
    return pl.pallas_call(
        body, grid=grid, in_specs=specs, out_specs=o_spec,
        out_shape=jax.ShapeDtypeStruct((m, n), F32),
        compiler_params=_cparams(2), name=name,
    )(*args)


def _mm_k_body(nk, a_ref, b_ref, r_ref, o_ref, acc_ref):
    kk = pl.program_id(2)

    @pl.when(kk == 0)
    def _():
        acc_ref[...] = r_ref[...]

    acc_ref[...] += jnp.dot(a_ref[...], b_ref[...], preferred_element_type=F32)

    @pl.when(kk == nk - 1)
    def _():
        o_ref[...] = acc_ref[...]


def _matmul_k(a, b, res, tk, name="matmul_k"):
    m, k = a.shape
    n = b.shape[1]
    nk = k // tk
    grid = (n // COL_TILE, m // ROW_TILE, nk)
    o_spec = pl.BlockSpec((ROW_TILE, COL_TILE), lambda j, i, kk: (i, j))
    return pl.pallas_call(
        functools.partial(_mm_k_body, nk), grid=grid,
        in_specs=[pl.BlockSpec((ROW_TILE, tk), lambda j, i, kk: (i, kk)),
                  pl.BlockSpec((tk, COL_TILE), lambda j, i, kk: (kk, j)), o_spec],
        out_specs=o_spec,
        out_shape=jax.ShapeDtypeStruct((m, n), F32),
        scratch_shapes=[pltpu.VMEM((ROW_TILE, COL_TILE), F32)],
        compiler_params=_cparams(3), name=name,
    )(a, b, res)


def _row_spec(width, col_start, seg):
    row_base, C, B, nblk = seg
    rb, cb = row_base // C, col_start // width
    return pl.BlockSpec((C, width), lambda b, i: (rb + b * nblk + i, cb))


def _state_spec(shape):
    nd = len(shape)
    return pl.BlockSpec((1,) + tuple(shape[1:]), lambda b, i: (b,) + (0,) * (nd - 1))


def _param_spec(shape):
    nd = len(shape)
    return pl.BlockSpec(tuple(shape), lambda b, i: (0,) * nd)


def _seq_call(body, seg, row_ins, state_ins, params, buf, out_width, out_col, state_out_dtypes, scratch, name):
    row_base, C, B, nblk = seg
    in_specs = [_row_spec(w, c, seg) for (_, w, c) in row_ins]
    in_specs += [_state_spec(s.shape) for s in state_ins]
    in_specs += [_param_spec(p.shape) for p in params]
    in_specs += [pl.BlockSpec(memory_space=pl.ANY)]
    args = [a for (a, _, _) in row_ins] + list(state_ins) + list(params) + [buf]
    out_specs = [_row_spec(out_width, out_col, seg)] + [_state_spec(s.shape) for s in state_ins]
    out_shape = [jax.ShapeDtypeStruct(buf.shape, buf.dtype)] + [jax.ShapeDtypeStruct(s.shape, s.dtype) for s in state_ins]
    return pl.pallas_call(
        functools.partial(body, C, nblk), grid=(B, nblk), in_specs=in_specs, out_specs=out_specs,
        out_shape=out_shape, scratch_shapes=scratch,
        input_output_aliases={len(args) - 1: 0},
        compiler_params=_cparams(2), name=name,
    )(*args)


def _rwkv_body(C, nblk, p_ref, sh_in, s_in, mu, w0, wup, a0, aup, gup, kkw, kaw, rkw, lnw, lnb, _buf,
               o_ref, sh_out, s_out, carry, s_scr):
    i = pl.program_id(1)

    @pl.when(i == 0)
    def _():
        carry[...] = sh_in[0]
        s_scr[...] = s_in[0]

    p = p_ref[...]
    rows = lax.broadcasted_iota(jnp.int32, (C, 1), 0)
    prev = jnp.where(rows == 0, carry[...], pltpu.roll(p, 1, 0))
    carry[...] = p[C - 1:C, :]
    xs = p + (prev - p) * mu[...]
    r, k, v = xs[:, 0:RW_W], xs[:, RW_W:2 * RW_W], xs[:, 2 * RW_W:3 * RW_W]
    dw, da, dg = xs[:, 3 * RW_W:3 * RW_W + 64], xs[:, 3 * RW_W + 64:3 * RW_W + 128], xs[:, 3 * RW_W + 128:RW_P]
    lw = -math.exp(-0.5) * _sigmoid(w0[...] + _bdot(jnp.tanh(dw), wup[...]))
    a = _sigmoid(a0[...] + _bdot(da, aup[...]))
    g = _bdot(_sigmoid(dg), gup[...])
    kkx = k * kkw[...]
    kt = k * (1.0 + (a - 1.0) * kaw[...])
    rk = r * kt * rkw[...]

    tril, strict, eye = _tri_masks(C)
    cl = _cumsum_rows(lw, tril.astype(BF16))
    w_in = jnp.exp(cl)
    w_ex = jnp.exp(cl - lw)
    w_inv = jnp.exp(-cl)
    w_end = jnp.exp(cl[C - 1:C, :] - cl)

    ys, bonus = [], []
    for h in range(RW_HEADS):
        sl = slice(RW_HD * h, RW_HD * (h + 1))
        kk = kkx[:, sl]
        kk = kk * lax.rsqrt(jnp.sum(kk * kk, -1, keepdims=True) + 1e-6)
        bb = kk * a[:, sl]
        v_h, kt_h = v[:, sl], kt[:, sl]
        lhs = jnp.concatenate([kk * w_ex[:, sl], r[:, sl] * w_in[:, sl]], axis=0)
        rhs = jnp.concatenate([kt_h * w_inv[:, sl], bb * w_inv[:, sl]], axis=0)
        sc = _bdot_nt(lhs, rhs)
        a_bk = jnp.where(strict, sc[:C, :C], 0.0)
        a_bb = jnp.where(strict, sc[:C, C:], 0.0)
        a_r = jnp.concatenate([jnp.where(tril, sc[C:, :C], 0.0), jnp.where(tril, sc[C:, C:], 0.0)], axis=1)
        s0 = s_scr[h]
        ps = _bdot_nt(lhs, s0)
        u = _bdot(_unit_lower_inverse(a_bb, eye, C), -(ps[:C] + _bdot(a_bk, v_h)))
        vu = jnp.concatenate([v_h, u], axis=0)
        y = ps[C:] + _bdot(a_r, vu)
        ke = jnp.concatenate([kt_h * w_end[:, sl], bb * w_end[:, sl]], axis=0)
        s_scr[h] = s0 * w_in[C - 1:C, sl] + _bdot_tn(vu, ke)
        mean = jnp.mean(y, -1, keepdims=True)
        yc = y - mean
        var = jnp.mean(yc * yc, -1, keepdims=True)
        ys.append(yc * lax.rsqrt(var + RW_GN_EPS))
        bonus.append(jnp.sum(rk[:, sl], -1, keepdims=True) * v_h)
    y = jnp.concatenate(ys, axis=1) * lnw[...] + lnb[...]
    o_ref[...] = ((y + jnp.concatenate(bonus, axis=1)) * g).astype(o_ref.dtype)

    @pl.when(i == nblk - 1)
    def _():
        sh_out[0] = p[C - 1:C, :]
        s_out[0] = s_scr[...]


def _rwkv_call(p, buf, seg, states, params):
    shift, s = states
    outs = _seq_call(
        _rwkv_body, seg, [(p, RW_P, COL_RW)], [shift, s], params, buf, RW_W, 0, None,
        [pltpu.VMEM((1, RW_P), F32), pltpu.VMEM((RW_HEADS, RW_HD, RW_HD), F32)], "rwkv_mix")
    return outs[0], (outs[1], outs[2])


def _gdn_body(C, nblk, qkv_ref, z_ref, ab_ref, tail_in, s_in, convw, alog, dtb, normw, _buf,
              o_ref, tail_out, s_out, tail, s_scr):
    i = pl.program_id(1)

    @pl.when(i == 0)
    def _():
        tail[...] = tail_in[0]
        s_scr[...] = s_in[0]

    x = qkv_ref[...]
    t8 = tail[...]
    conv = x * convw[3:4, :]
    for s in (1, 2, 3):
        conv = conv + _shift_rows(x, t8, s, C) * convw[3 - s:4 - s, :]
    tail[...] = x[C - TAIL:, :]
    act = conv * _sigmoid(conv)
    q, k, v = act[:, 0:GD_W], act[:, GD_W:2 * GD_W], act[:, 2 * GD_W:3 * GD_W]

    ab = ab_ref[...]
    g_all = -jnp.exp(alog[...]) * _softplus(ab + dtb[...])
    beta_all = _sigmoid(ab)
    tril, strict, eye = _tri_masks(C)
    gc = _cumsum_rows(g_all, tril.astype(BF16))
    gr = _cols_to_rows(gc, 16)
    eg = jnp.exp(gc)

    outs = []
    for h in range(GD_HEADS):
        sl = slice(GD_HD * h, GD_HD * (h + 1))
        q_h, k_h, v_h = q[:, sl], k[:, sl], v[:, sl]
        q_h = q_h * (lax.rsqrt(jnp.sum(q_h * q_h, -1, keepdims=True) + 1e-6) * GD_HD ** -0.5)
        k_h = k_h * lax.rsqrt(jnp.sum(k_h * k_h, -1, keepdims=True) + 1e-6)
        gc_h, gr_h = gc[:, h:h + 1], gr[h:h + 1, :]
        beta = beta_all[:, GD_HEADS + h:GD_HEADS + h + 1]
        eg_h = eg[:, h:h + 1]
        decay = jnp.where(tril, jnp.exp(jnp.where(tril, gc_h - gr_h, 0.0)), 0.0)
        sc = _bdot_nt(jnp.concatenate([k_h, q_h], axis=0), k_h)
        a_kk = jnp.where(strict, beta * sc[:C] * decay, 0.0)
        qk = sc[C:] * decay
        rhs = jnp.concatenate([v_h * beta, k_h * (beta * eg_h)], axis=1)
        sol = _bdot(_unit_lower_inverse(a_kk, eye, C), rhs)
        s0 = s_scr[h]
        ps = _bdot_nt(jnp.concatenate([sol[:, GD_HD:], q_h], axis=0), s0)
        u = sol[:, :GD_HD] - ps[:C]
        o = eg_h * ps[C:] + _bdot(qk, u)
        gl = gc_h[C - 1:C, :]
        s_scr[h] = jnp.exp(gl) * s0 + _bdot_tn(u * jnp.exp(gl - gc_h), k_h)
        outs.append(o * lax.rsqrt(jnp.mean(o * o, -1, keepdims=True) + NORM_EPS) * normw[...])
    z = z_ref[...]
    o_ref[...] = (jnp.concatenate(outs, axis=1) * (z * _sigmoid(z))).astype(o_ref.dtype)

    @pl.when(i == nblk - 1)
    def _():
        tail_out[0] = tail[...]
        s_out[0] = s_scr[...]


def _gdn_call(p, buf, seg, states, params):
    tail, s = states
    outs = _seq_call(
        _gdn_body, seg, [(p, 3 * GD_W, COL_GD_QKV), (p, GD_W, COL_GD_Z), (p, LANE, COL_GD_AB)], [tail, s], params,
        buf, GD_W, RW_W, None,
        [pltpu.VMEM((TAIL, 3 * GD_W), F32), pltpu.VMEM((GD_HEADS, GD_HD, GD_HD), F32)], "gdn_mix")
    return outs[0], (outs[1], outs[2])


def _mlstm_body(C, nblk, q_ref, k_ref, v_ref, og_ref, if_ref, c_in, n_in, m_in, ib, fb, normw, _buf,
                o_ref, c_out, n_out, m_out, c_scr, n_scr, m_scr):
    i = pl.program_id(1)

    @pl.when(i == 0)
    def _():
        c_scr[...] = c_in[0]
        n_scr[...] = n_in[0]
        m_scr[...] = m_in[0]

    q, k, v = q_ref[...], k_ref[...], v_ref[...]
    gates = if_ref[...]
    li_all = gates + ib[...]
    x = gates + fb[...]
    lf_all = jnp.minimum(x, 0.0) - jnp.log(1.0 + jnp.exp(-jnp.abs(x)))
    tril, _, _ = _tri_masks(C)
    fc = _cumsum_rows(lf_all, tril.astype(BF16))
    fr = _cols_to_rows(fc, 8)
    lir = _cols_to_rows(li_all, 8)
    lane = lax.broadcasted_iota(jnp.int32, (1, LANE), 1)
    m_all = m_scr[...]
    m_new = m_all

    outs = []
    for h in range(ML_HEADS):
        q_h = q[:, ML_DK * h:ML_DK * (h + 1)] * ML_DK ** -0.5
        k_h = k[:, ML_DK * h:ML_DK * (h + 1)]
        v_h = v[:, ML_DV * h:ML_DV * (h + 1)]
        fc_h, fr_h = fc[:, ML_HEADS + h:ML_HEADS + h + 1], fr[ML_HEADS + h:ML_HEADS + h + 1, :]
        li_c, li_r = li_all[:, h:h + 1], lir[h:h + 1, :]
        m0 = m_all[:, h:h + 1]
        dmat = jnp.where(tril, fc_h - fr_h + li_r, -jnp.inf)
        inter = fc_h + m0
        m = jnp.maximum(inter, jnp.max(dmat, -1, keepdims=True))
        dw = jnp.exp(dmat - m)
        wi = jnp.exp(inter - m)
        s = _bdot_nt(q_h, k_h) * dw
        c0 = c_scr[h]
        n0 = n_scr[h:h + 1, :]
        num = wi * _bdot_nt(q_h, c0) + _bdot(s, v_h)
        den = wi * jnp.sum(q_h * n0, -1, keepdims=True) + jnp.sum(s, -1, keepdims=True)
        hh = num / jnp.maximum(jnp.abs(den), jnp.exp(-m))
        m_c = m[C - 1:C, :]
        f_last = fc_h[C - 1:C, :]
        wend = jnp.exp(f_last - fc_h + li_c - m_c)
        dstate = jnp.exp(f_last + m0 - m_c)
        c_scr[h] = dstate * c0 + _bdot_tn(v_h * wend, k_h)
        n_scr[h:h + 1, :] = dstate * n0 + jnp.sum(wend * k_h, 0, keepdims=True)
        m_new = jnp.where(lane == h, m_c, m_new)
        outs.append(hh * lax.rsqrt(jnp.mean(hh * hh, -1, keepdims=True) + NORM_EPS))
    m_scr[...] = m_new
    o_ref[...] = (jnp.concatenate(outs, axis=1) * normw[...] * _sigmoid(og_ref[...])).astype(o_ref.dtype)

    @pl.when(i == nblk - 1)
    def _():
        c_out[0] = c_scr[...]
        n_out[0] = n_scr[...]
        m_out[0] = m_new


def _mlstm_call(p, buf, seg, states, params):
    outs = _seq_call(
        _mlstm_body, seg,
        [(p, ML_QK, COL_ML_Q), (p, ML_QK, COL_ML_K), (p, ML_W, COL_ML_V), (p, ML_W, COL_ML_O), (p, LANE, COL_ML_IF)],
        list(states), params, buf, ML_W, RW_W + GD_W, None,
        [pltpu.VMEM((ML_HEADS, ML_DV, ML_DK), F32), pltpu.VMEM((ML_HEADS, ML_DK), F32), pltpu.VMEM((1, LANE), F32)],
        "mlstm_mix")
    return outs[0], tuple(outs[1:])


def _ffn_act_body(C, nblk, g_ref, u_ref, tail_in, convw, convb, _buf, o_ref, tail_out, tail):
    i = pl.program_id(1)

    @pl.when(i == 0)
    def _():
        tail[...] = tail_in[0]

    x = g_ref[...]
    t8 = tail[...]
    conv = x * convw[2:3, :] + convb[...]
    for s in (1, 2):
        conv = conv + _shift_rows(x, t8, s, C) * convw[2 - s:3 - s, :]
    tail[...] = x[C - TAIL:, :]
    o_ref[...] = (conv * _sigmoid(conv) * u_ref[...]).astype(o_ref.dtype)

    @pl.when(i == nblk - 1)
    def _():
        tail_out[0] = tail[...]


def _ffn_act_call(gu, buf, seg, tail, params):
    outs = _seq_call(
        _ffn_act_body, seg, [(gu, FF_PAD, 0), (gu, FF_PAD, FF_PAD)], [tail], params, buf, FF_PAD, 0, None,
        [pltpu.VMEM((TAIL, FF_PAD), F32)], "ffn_act")
    return outs[0], outs[1]


def _pad_cols(x, width):
    return jnp.pad(x, ((0, 0), (0, width - x.shape[1])))


def _lane_vec(x, start=0):
    return jnp.zeros((1, LANE), F32).at[0, start:start + x.shape[0]].set(x)


def _permute_cols(w):
    gd0, ml0 = RW_P, RW_P + GD_P
    gd_ab = w[:, gd0 + 3 * GD_W:gd0 + 3 * GD_W + 2 * GD_HEADS]
    ml_if = w[:, ml0 + 2 * ML_QK + ML_W:ml0 + 2 * ML_QK + ML_W + 2 * ML_HEADS]
    cols = [w[:, gd0:gd0 + 3 * GD_W], _pad_cols(gd_ab, LANE), _pad_cols(ml_if, LANE), w[:, :RW_P],
            w[:, ml0:ml0 + 2 * ML_QK], w[:, gd0 + 3 * GD_W + 2 * GD_HEADS:gd0 + GD_P],
            w[:, ml0 + 2 * ML_QK:ml0 + 2 * ML_QK + ML_W], w[:, ml0 + 2 * ML_QK + ML_W + 2 * ML_HEADS:]]
    return jnp.concatenate(cols, axis=1)


def _pad_tail(buf, width):
    b, r, w = buf.shape
    return jnp.pad(buf, ((0, 0), (TAIL - r, 0), (0, width - w)))


def _layer(x, hb_norm_w, states, lp, segs):
    n = x.shape[0]
    hb = _rmsnorm(x, hb_norm_w, BF16, n, 0, 256)
    p = _matmul(hb, lp["w_in"], name="proj_in")
    mix = jnp.zeros((n, D_MODEL), BF16)
    new_states = []
    for seg, st in zip(segs, states):
        rw_st, gd_st, ml_st, ffn_tail = st
        if rw_st is None:
            rw_st, gd_st, ml_st, ffn_tail = new_states[-1][:4]
        mix, rw_new = _rwkv_call(p, mix, seg, rw_st, lp["rwkv"])
        mix, gd_new = _gdn_call(p, mix, seg, gd_st, lp["gdn"])
        mix, ml_new = _mlstm_call(p, mix, seg, ml_st, lp["mlstm"])
        new_states.append([rw_new, gd_new, ml_new, ffn_tail])
    x = _matmul(mix, lp["w_out"], res=x, name="proj_out")
    hb = _rmsnorm(x, lp["norm_ffn_w"], BF16, n, 0, 256)
    gu = _matmul(hb, lp["w_gu"], name="ffn_gate_up")
    act = jnp.zeros((n, FF_PAD), BF16)
    for idx, seg in enumerate(segs):
        act, tail_new = _ffn_act_call(gu, act, seg, new_states[idx][3], lp["ffn"])
        new_states[idx][3] = tail_new
        if idx + 1 < len(segs) and states[idx + 1][0] is None:
            new_states[idx + 1][3] = tail_new
    x = _matmul_k(act, lp["w_down"], x, FF_KTILE, name="ffn_down")
    return x, new_states


def kernel(x_prompt, x_sample, state_rwkv_wkv, state_rwkv_shift, state_gdn, cache_gdn_conv, state_mlstm_c, state_mlstm_n, state_mlstm_m, cache_ffn_conv, meta_tokens, norm_mix_w, w_in, rwkv_mu, rwkv_w0, rwkv_w_up, rwkv_a0, rwkv_a_up, rwkv_g_up, rwkv_k_k, rwkv_k_a, rwkv_r_k, rwkv_ln_w, rwkv_ln_b, gdn_conv_w, gdn_a_log, gdn_dt_bias, gdn_norm_w, mlstm_i_b, mlstm_f_b, mlstm_norm_w, w_out, norm_ffn_w, ffn_w_gate, ffn_w_up, ffn_conv_w, ffn_conv_b, ffn_w_down, final_norm_w):
    depth = w_in.shape[0]
    seq = x_prompt.shape[1]
    dec_b, dec_seq = x_sample.shape[0], x_sample.shape[1]
    row_sample = ROW_PROMPT + seq
    n_rows = row_sample + dec_b * dec_seq
    n_pad = -(-n_rows // ROW_TILE) * ROW_TILE
    segs = ((ROW_META, N_META, 1, 1), (ROW_PROMPT, CHUNK, 1, seq // CHUNK), (row_sample, dec_seq, dec_b, 1))

    x = jnp.concatenate([
        jnp.zeros((ROW_META, D_MODEL), F32), meta_tokens.astype(F32), x_prompt[0],
        x_sample.reshape(dec_b * dec_seq, D_MODEL), jnp.zeros((n_pad - n_rows, D_MODEL), F32)], axis=0)

    row = lambda a: a.reshape(1, -1)
    zero_states = ((jnp.zeros((1, 1, RW_P), F32), jnp.zeros((1, RW_HEADS, RW_HD, RW_HD), F32)),
                   (jnp.zeros((1, TAIL, 3 * GD_W), F32), jnp.zeros((1, GD_HEADS, GD_HD, GD_HD), F32)),
                   (jnp.zeros((1, ML_HEADS, ML_DV, ML_DK), F32), jnp.zeros((1, ML_HEADS, ML_DK), F32),
                    jnp.zeros((1, 1, LANE), F32)),
                   jnp.zeros((1, TAIL, FF_PAD), F32))
    chained = (None, None, None, None)

    p_out, s_out = [], []
    for l in range(depth):
        lp = {
            "w_in": _permute_cols(w_in[l]).astype(BF16),
            "rwkv": [row(rwkv_mu[l]), row(rwkv_w0[l]), rwkv_w_up[l], row(rwkv_a0[l]), rwkv_a_up[l], rwkv_g_up[l],
                     row(rwkv_k_k[l]), row(rwkv_k_a[l]), row(rwkv_r_k[l]), row(rwkv_ln_w[l]), row(rwkv_ln_b[l])],
            "gdn": [gdn_conv_w[l], _lane_vec(gdn_a_log[l]), _lane_vec(gdn_dt_bias[l]), row(gdn_norm_w[l])],
            "mlstm": [_lane_vec(mlstm_i_b[l]), _lane_vec(mlstm_f_b[l], ML_HEADS), row(mlstm_norm_w[l])],
            "w_out": w_out[l].astype(BF16),
            "norm_ffn_w": norm_ffn_w[l],
            "w_gu": jnp.concatenate([_pad_cols(ffn_w_gate[l], FF_PAD), _pad_cols(ffn_w_up[l], FF_PAD)], axis=1).astype(BF16),
            "ffn": [_pad_cols(ffn_conv_w[l], FF_PAD), _pad_cols(row(ffn_conv_b[l]), FF_PAD)],
            "w_down": jnp.pad(ffn_w_down[l], ((0, FF_PAD - D_FF), (0, 0))).astype(BF16),
        }
        sample_states = ((state_rwkv_shift[l][:, None, :], state_rwkv_wkv[l]),
                         (_pad_tail(cache_gdn_conv[l], 3 * GD_W), state_gdn[l]),
                         (state_mlstm_c[l], state_mlstm_n[l], _pad_cols(state_mlstm_m[l], LANE)[:, None, :]),
                         _pad_tail(cache_ffn_conv[l], FF_PAD))
        x, st = _layer(x, norm_mix_w[l], (zero_states, chained, sample_states), lp, segs)
        p_out.append(st[1])
        s_out.append(st[2])

    y_prompt = _rmsnorm(x, final_norm_w, F32, seq, ROW_PROMPT, CHUNK).reshape(1, seq, D_MODEL)
    y_sample = _rmsnorm(x, final_norm_w, F32, dec_b * dec_seq, row_sample, CHUNK).reshape(dec_b, dec_seq, D_MODEL)

    def collect(sts):
        stack = lambda f: jnp.stack([f(st) for st in sts], 0)
        return (stack(lambda st: st[0][1]), stack(lambda st: st[0][0][:, 0, :]),
                stack(lambda st: st[1][1]), stack(lambda st: st[1][0][:, TAIL - 3:, :]),
                stack(lambda st: st[2][0]), stack(lambda st: st[2][1]), stack(lambda st: st[2][2][:, 0, :ML_HEADS]),
                stack(lambda st: st[3][:, TAIL - 2:, :D_FF]))

    return (y_prompt, y_sample) + collect(p_out) + collect(s_out)
```

```python
import functools
import math

import jax
import jax.numpy as jnp
from jax import lax
from jax.experimental import pallas as pl
from jax.experimental.pallas import tpu as pltpu

F32 = jnp.float32
BF16 = jnp.bfloat16

D_MODEL = 4096
N_META = 16
CHUNK = 64
NORM_EPS = 1e-6
RW_HEADS, RW_HD = 24, 64
RW_W = RW_HEADS * RW_HD
RW_P = 3 * RW_W + 64 + 64 + 128
RW_GN_EPS = 64e-5
GD_HEADS, GD_HD = 12, 128
GD_W = GD_HEADS * GD_HD
GD_P = 4 * GD_W + 2 * GD_HEADS
ML_HEADS, ML_DK, ML_DV = 4, 128, 256
ML_W = ML_HEADS * ML_DV
ML_QK = ML_HEADS * ML_DK
D_FF = 11008

LANE = 128
TAIL = 8
ROW_META = 48
ROW_PROMPT = 64
ROW_TILE = 2176
COL_TILE = 256
FFN_TAIL_ROWS = 512
VMEM_LIMIT = 56 * 1024 * 1024

COL_GD_QKV = 0
COL_GD_AB = 4608
COL_ML_IF = 4736
COL_RW = 4864
COL_ML_Q = 9728
COL_ML_K = 10240
COL_GD_Z = 10752
COL_ML_V = 12288
COL_ML_O = 13312


def _cparams(n_axes):
    return pltpu.CompilerParams(dimension_semantics=("arbitrary",) * n_axes, vmem_limit_bytes=VMEM_LIMIT)


def _bdot(a, b):
    return jnp.dot(a.astype(BF16), b.astype(BF16), preferred_element_type=F32)


def _bdot_nt(a, b):
    return lax.dot_general(a.astype(BF16), b.astype(BF16), (((1,), (1,)), ((), ())), preferred_element_type=F32)


def _bdot_tn(a, b):
    return lax.dot_general(a.astype(BF16), b.astype(BF16), (((0,), (0,)), ((), ())), preferred_element_type=F32)


def _split3(x):
    hi = x.astype(BF16)
    r1 = x - hi.astype(F32)
    mid = r1.astype(BF16)
    lo = (r1 - mid.astype(F32)).astype(BF16)
    return hi, mid, lo


def _cumsum_rows(x, tril_bf):
    hi, mid, lo = _split3(x)
    d = lambda t: jnp.dot(tril_bf, t, preferred_element_type=F32)
    return d(hi) + d(mid) + d(lo)


def _cols_to_rows(x, n_rows):
    sel = (lax.broadcasted_iota(jnp.int32, (n_rows, LANE), 0) == lax.broadcasted_iota(jnp.int32, (n_rows, LANE), 1)).astype(BF16)
    hi, mid, lo = _split3(x)
    d = lambda t: lax.dot_general(sel, t, (((1,), (1,)), ((), ())), preferred_element_type=F32)
    return d(hi) + d(mid) + d(lo)


def _head_selectors(width, hd):
    sh = int(math.log2(hd))
    e = (lax.broadcasted_iota(jnp.int32, (width, LANE), 0) >> sh) == lax.broadcasted_iota(jnp.int32, (width, LANE), 1)
    et = lax.broadcasted_iota(jnp.int32, (LANE, width), 0) == (lax.broadcasted_iota(jnp.int32, (LANE, width), 1) >> sh)
    return e.astype(BF16), et.astype(BF16)


def _head_sums(x, sel):
    e, et = sel
    hi, lo, _ = _split3(x)
    d = lambda t: jnp.dot(t, e, preferred_element_type=F32)
    hi, lo, _ = _split3(d(hi) + d(lo))
    d = lambda t: jnp.dot(t, et, preferred_element_type=F32)
    return d(hi) + d(lo)


def _tri_masks(C):
    ri = lax.broadcasted_iota(jnp.int32, (C, C), 0)
    ci = lax.broadcasted_iota(jnp.int32, (C, C), 1)
    return ri >= ci, ri > ci, ri == ci


def _unit_lower_inverse(As, eye, C):
    Ps = [-A for A in As]
    Ts = [jnp.where(eye, 1.0, P) for P in Ps]
    for _ in range(int(math.log2(C)) - 1):
        Ps = [_bdot(P, P) for P in Ps]
        Ts = [T + _bdot(T, P) for T, P in zip(Ts, Ps)]
    return Ts


def _sigmoid(x):
    return jax.nn.sigmoid(x)


def _softplus(x):
    return jnp.maximum(x, 0.0) + jnp.log(1.0 + jnp.exp(-jnp.abs(x)))


def _shift_rows(x, tail, s, C):
    xr = pltpu.roll(x, s, 0)
    r8 = lax.broadcasted_iota(jnp.int32, (TAIL, 1), 0)
    top = jnp.where(r8 < s, pltpu.roll(tail, s, 0), xr[0:TAIL])
    return jnp.concatenate([top, xr[TAIL:]], axis=0)


def _rmsnorm_body(x_ref, w_ref, o_ref):
    x = x_ref[...]
    y = x * lax.rsqrt(jnp.mean(x * x, -1, keepdims=True) + NORM_EPS)
    o_ref[...] = (y * w_ref[...]).astype(o_ref.dtype)


def _rmsnorm(x, w, out_dtype, rows, row_base, block):
    d = x.shape[1]
    rb = row_base // block
    return pl.pallas_call(
        _rmsnorm_body,
        grid=(rows // block,),
        in_specs=[pl.BlockSpec((block, d), lambda i: (rb + i, 0)), pl.BlockSpec((1, d), lambda i: (0, 0))],
        out_specs=pl.BlockSpec((block, d), lambda i: (i, 0)),
        out_shape=jax.ShapeDtypeStruct((rows, d), out_dtype),
        compiler_params=_cparams(1),
        name="rmsnorm",
    )(x, w.reshape(1, d))


def _mm_body(a_ref, w_ref, o_ref):
    o_ref[...] = jnp.dot(a_ref[...], w_ref[...].astype(BF16), preferred_element_type=F32)


def _mm_res_body(a_ref, w_ref, r_ref, o_ref):
    o_ref[...] = r_ref[...] + jnp.dot(a_ref[...], w_ref[...].astype(BF16), preferred_element_type=F32)


def _matmul_wide(a, w, layer, name="matmul_wide"):
    m, k = a.shape
    n = w.shape[2]
    tm, tn = ROW_TILE // 4, 4 * COL_TILE
    return pl.pallas_call(
        _mm_body, grid=(n // tn, m // tm),
        in_specs=[pl.BlockSpec((tm, k), lambda j, i: (i, 0)), pl.BlockSpec((None, k, tn), lambda j, i: (layer, 0, j))],
        out_specs=pl.BlockSpec((tm, tn), lambda j, i: (i, j)),
        out_shape=jax.ShapeDtypeStruct((m, n), F32),
        compiler_params=_cparams(2), name=name,
    )(a, w)


def _matmul(a, w, layer, tm, res=None, name="matmul"):
    m, k = a.shape
    n = w.shape[2]
    a_spec = pl.BlockSpec((tm, k), lambda i, j: (i, 0), pipeline_mode=pl.Buffered(1))
    w_spec = pl.BlockSpec((None, k, COL_TILE), lambda i, j: (layer, 0, j))
    o_spec = pl.BlockSpec((tm, COL_TILE), lambda i, j: (i, j))
    if res is None:
        body, specs, args = _mm_body, [a_spec, w_spec], (a, w)
    else:
        body, specs, args = _mm_res_body, [a_spec, w_spec, o_spec], (a, w, res)
    return pl.pallas_call(
        body, grid=(m // tm, n // COL_TILE), in_specs=specs, out_specs=o_spec,
        out_shape=jax.ShapeDtypeStruct((m, n), F32),
        compiler_params=_cparams(2), name=name,
    )(*args)


def _row_spec(width, col_start, seg):
    row_base, C, B, nblk = seg
    rb, cb = row_base // C, col_start // width
    return pl.BlockSpec((C, width), lambda b, i: (rb + b * nblk + i, cb))


def _state_spec(shape):
    nd = len(shape)
    return pl.BlockSpec((1,) + tuple(shape[1:]), lambda b, i: (b,) + (0,) * (nd - 1))


def _param_spec(shape):
    nd = len(shape)
    return pl.BlockSpec(tuple(shape), lambda b, i: (0,) * nd)


def _seq_call(body, seg, row_ins, state_ins, params, buf, out_width, out_col, scratch, name, in_row_base=None):
    row_base, C, B, nblk = seg
    in_seg = seg if in_row_base is None else (in_row_base, C, B, nblk)
    in_specs = [_row_spec(w, c, in_seg) for (_, w, c) in row_ins]
    in_specs += [_state_spec(s.shape) for s in state_ins]
    in_specs += [_param_spec(p.shape) for p in params]
    in_specs += [pl.BlockSpec(memory_space=pl.ANY)]
    args = [a for (a, _, _) in row_ins] + list(state_ins) + list(params) + [buf]
    out_specs = [_row_spec(out_width, out_col, seg)] + [_state_spec(s.shape) for s in state_ins]
    out_shape = [jax.ShapeDtypeStruct(buf.shape, buf.dtype)] + [jax.ShapeDtypeStruct(s.shape, s.dtype) for s in state_ins]
    return pl.pallas_call(
        functools.partial(body, C, nblk), grid=(B, nblk), in_specs=in_specs, out_specs=out_specs,
        out_shape=out_shape, scratch_shapes=scratch,
        input_output_aliases={len(args) - 1: 0},
        compiler_params=_cparams(2), name=name,
    )(*args)


def _rwkv_body(C, nblk, p_ref, sh_in, s_in, mu, w0, wup, a0, aup, gup, kkw, kaw, rkw, lnw, lnb, _buf,
               o_ref, sh_out, s_out, carry, s_scr):
    i = pl.program_id(1)

    @pl.when(i == 0)
    def _():
        carry[...] = sh_in[0]
        s_scr[...] = s_in[0]

    p = p_ref[...]
    rows = lax.broadcasted_iota(jnp.int32, (C, 1), 0)
    prev = jnp.where(rows == 0, carry[...], pltpu.roll(p, 1, 0))
    carry[...] = p[C - 1:C, :]
    xs = p + (prev - p) * mu[...]
    r, k, v = xs[:, 0:RW_W], xs[:, RW_W:2 * RW_W], xs[:, 2 * RW_W:3 * RW_W]
    dw, da, dg = xs[:, 3 * RW_W:3 * RW_W + 64], xs[:, 3 * RW_W + 64:3 * RW_W + 128], xs[:, 3 * RW_W + 128:RW_P]
    lw = -math.exp(-0.5) * _sigmoid(w0[...] + _bdot(jnp.tanh(dw), wup[...]))
    a = _sigmoid(a0[...] + _bdot(da, aup[...]))
    g = _bdot(_sigmoid(dg), gup[...])
    sel = _head_selectors(RW_W, RW_HD)
    kkx = k * kkw[...]
    kkn = kkx * lax.rsqrt(_head_sums(kkx * kkx, sel) + 1e-6)
    kt = k * (1.0 + (a - 1.0) * kaw[...])
    bonus = _head_sums(r * kt * rkw[...], sel) * v
    kb = kkn * a

    tril, strict, eye = _tri_masks(C)
    cl = _cumsum_rows(lw, tril.astype(BF16))
    w_in = jnp.exp(cl)
    w_ex = jnp.exp(cl - lw)
    w_inv = jnp.exp(-cl)
    w_end = jnp.exp(cl[C - 1:C, :] - cl)

    heads = range(RW_HEADS)
    sls = [slice(RW_HD * h, RW_HD * (h + 1)) for h in heads]
    kk_ex, r_in = kkn * w_ex, r * w_in
    kt_inv, kb_inv = kt * w_inv, kb * w_inv
    kt_end, kb_end = kt * w_end, kb * w_end
    lhss = [jnp.concatenate([kk_ex[:, sl], r_in[:, sl]], axis=0) for sl in sls]
    rhss = [jnp.concatenate([kt_inv[:, sl], kb_inv[:, sl]], axis=0) for sl in sls]
    scs = [_bdot_nt(lhs, rhs) for lhs, rhs in zip(lhss, rhss)]
    s0s = [s_scr[h] for h in heads]
    pss = [_bdot_nt(lhs, s0) for lhs, s0 in zip(lhss, s0s)]
    negs = [-(ps[:C] + _bdot(jnp.where(strict, sc[:C, :C], 0.0), v[:, sl])) for ps, sc, sl in zip(pss, scs, sls)]
    invs = _unit_lower_inverse([jnp.where(strict, sc[:C, C:], 0.0) for sc in scs], eye, C)
    us = [_bdot(t, n) for t, n in zip(invs, negs)]
    vus = [jnp.concatenate([v[:, sl], u], axis=0) for u, sl in zip(us, sls)]
    a_rs = [jnp.concatenate([jnp.where(tril, sc[C:, :C], 0.0), jnp.where(tril, sc[C:, C:], 0.0)], axis=1) for sc in scs]
    ys = [ps[C:] + _bdot(a_r, vu) for ps, a_r, vu in zip(pss, a_rs, vus)]
    kes = [jnp.concatenate([kt_end[:, sl], kb_end[:, sl]], axis=0) for sl in sls]
    for h in heads:
        s_scr[h] = s0s[h] * w_in[C - 1:C, sls[h]] + _bdot_tn(vus[h], kes[h])
    y = jnp.concatenate(ys, axis=1)
    yc = y - _head_sums(y, sel) * (1.0 / RW_HD)
    yn = yc * lax.rsqrt(_head_sums(yc * yc, sel) * (1.0 / RW_HD) + RW_GN_EPS)
    o_ref[...] = ((yn * lnw[...] + lnb[...] + bonus) * g).astype(o_ref.dtype)

    @pl.when(i == nblk - 1)
    def _():
        sh_out[0] = p[C - 1:C, :]
        s_out[0] = s_scr[...]


def _rwkv_call(p, buf, seg, states, params):
    shift, s = states
    outs = _seq_call(
        _rwkv_body, seg, [(p, RW_P, COL_RW)], [shift, s], params, buf, RW_W, 0,
        [pltpu.VMEM((1, RW_P), F32), pltpu.VMEM((RW_HEADS, RW_HD, RW_HD), F32)], "rwkv_mix")
    return outs[0], (outs[1], outs[2])


def _gdn_body(C, nblk, qkv_ref, z_ref, ab_ref, tail_in, s_in, convw, alog, dtb, normw, _buf,
              o_ref, tail_out, s_out, tail, s_scr):
    i = pl.program_id(1)

    @pl.when(i == 0)
    def _():
        tail[...] = tail_in[0]
        s_scr[...] = s_in[0]

    x = qkv_ref[...]
    t8 = tail[...]
    conv = x * convw[3:4, :]
    for s in (1, 2, 3):
        conv = conv + _shift_rows(x, t8, s, C) * convw[3 - s:4 - s, :]
    tail[...] = x[C - TAIL:, :]
    act = conv * _sigmoid(conv)
    q, k, v = act[:, 0:GD_W], act[:, GD_W:2 * GD_W], act[:, 2 * GD_W:3 * GD_W]

    ab = ab_ref[...]
    g_all = -jnp.exp(alog[...]) * _softplus(ab + dtb[...])
    beta_all = _sigmoid(ab)
    tril, strict, eye = _tri_masks(C)
    gc = _cumsum_rows(g_all, tril.astype(BF16))
    gr = _cols_to_rows(gc, 16)
    eg = jnp.exp(gc)

    heads = range(GD_HEADS)
    sls = [slice(GD_HD * h, GD_HD * (h + 1)) for h in heads]
    qs = [q[:, sl] for sl in sls]
    qs = [t * (lax.rsqrt(jnp.sum(t * t, -1, keepdims=True) + 1e-6) * GD_HD ** -0.5) for t in qs]
    ks = [k[:, sl] for sl in sls]
    ks = [t * lax.rsqrt(jnp.sum(t * t, -1, keepdims=True) + 1e-6) for t in ks]
    betas = [beta_all[:, GD_HEADS + h:GD_HEADS + h + 1] for h in heads]
    decays = [jnp.where(tril, jnp.exp(jnp.where(tril, gc[:, h:h + 1] - gr[h:h + 1, :], 0.0)), 0.0) for h in heads]
    scs = [_bdot_nt(jnp.concatenate([k_h, q_h], axis=0), k_h) for k_h, q_h in zip(ks, qs)]
    invs = _unit_lower_inverse(
        [jnp.where(strict, beta * sc[:C] * decay, 0.0) for beta, sc, decay in zip(betas, scs, decays)], eye, C)
    rhss = [jnp.concatenate([v[:, sl] * beta, k_h * (beta * eg[:, h:h + 1])], axis=1)
            for h, sl, beta, k_h in zip(heads, sls, betas, ks)]
    sols = [_bdot(t, rhs) for t, rhs in zip(invs, rhss)]
    s0s = [s_scr[h] for h in heads]
    pss = [_bdot_nt(jnp.concatenate([sol[:, GD_HD:], q_h], axis=0), s0) for sol, q_h, s0 in zip(sols, qs, s0s)]
    us = [sol[:, :GD_HD] - ps[:C] for sol, ps in zip(sols, pss)]
    os_ = [eg[:, h:h + 1] * ps[C:] + _bdot(sc[C:] * decay, u) for h, ps, sc, decay, u in zip(heads, pss, scs, decays, us)]
    for h in heads:
        gl = gc[C - 1:C, h:h + 1]
        s_scr[h] = jnp.exp(gl) * s0s[h] + _bdot_tn(us[h] * jnp.exp(gl - gc[:, h:h + 1]), ks[h])
    outs = [o * lax.rsqrt(jnp.mean(o * o, -1, keepdims=True) + NORM_EPS) * normw[...] for o in os_]
    z = z_ref[...]
    o_ref[...] = (jnp.concatenate(outs, axis=1) * (z * _sigmoid(z))).astype(o_ref.dtype)

    @pl.when(i == nblk - 1)
    def _():
        tail_out[0] = tail[...]
        s_out[0] = s_scr[...]


def _gdn_call(p, buf, seg, states, params):
    tail, s = states
    outs = _seq_call(
        _gdn_body, seg, [(p, 3 * GD_W, COL_GD_QKV), (p, GD_W, COL_GD_Z), (p, LANE, COL_GD_AB)], [tail, s], params,
        buf, GD_W, RW_W,
        [pltpu.VMEM((TAIL, 3 * GD_W), F32), pltpu.VMEM((GD_HEADS, GD_HD, GD_HD), F32)], "gdn_mix")
    return outs[0], (outs[1], outs[2])


def _mlstm_body(C, nblk, q_ref, k_ref, v_ref, og_ref, if_ref, c_in, n_in, m_in, ib, fb, normw, _buf,
                o_ref, c_out, n_out, m_out, c_scr, n_scr, m_scr):
    i = pl.program_id(1)

    @pl.when(i == 0)
    def _():
        c_scr[...] = c_in[0]
        n_scr[...] = n_in[0]
        m_scr[...] = m_in[0]

    q, k, v = q_ref[...], k_ref[...], v_ref[...]
    gates = if_ref[...]
    li_all = gates + ib[...]
    x = gates + fb[...]
    lf_all = jnp.minimum(x, 0.0) - jnp.log(1.0 + jnp.exp(-jnp.abs(x)))
    tril, _, _ = _tri_masks(C)
    fc = _cumsum_rows(lf_all, tril.astype(BF16))
    fr = _cols_to_rows(fc, 8)
    lir = _cols_to_rows(li_all, 8)
    lane = lax.broadcasted_iota(jnp.int32, (1, LANE), 1)
    m_all = m_scr[...]
    m_new = m_all

    heads = range(ML_HEADS)
    qs = [q[:, ML_DK * h:ML_DK * (h + 1)] * ML_DK ** -0.5 for h in heads]
    ks = [k[:, ML_DK * h:ML_DK * (h + 1)] for h in heads]
    vs = [v[:, ML_DV * h:ML_DV * (h + 1)] for h in heads]
    c0s = [c_scr[h] for h in heads]
    n0s = [n_scr[h:h + 1, :] for h in heads]
    qks = [_bdot_nt(q_h, k_h) for q_h, k_h in zip(qs, ks)]
    qcs = [_bdot_nt(q_h, c0) for q_h, c0 in zip(qs, c0s)]
    fcs = [fc[:, ML_HEADS + h:ML_HEADS + h + 1] for h in heads]
    m0s = [m_all[:, h:h + 1] for h in heads]
    dmats = [jnp.where(tril, fc_h - fr[ML_HEADS + h:ML_HEADS + h + 1, :] + lir[h:h + 1, :], -jnp.inf)
             for h, fc_h in zip(heads, fcs)]
    inters = [fc_h + m0 for fc_h, m0 in zip(fcs, m0s)]
    ms = [jnp.maximum(inter, jnp.max(dmat, -1, keepdims=True)) for inter, dmat in zip(inters, dmats)]
    ss = [qk * jnp.exp(dmat - m) for qk, dmat, m in zip(qks, dmats, ms)]
    wis = [jnp.exp(inter - m) for inter, m in zip(inters, ms)]
    nums = [wi * qc + _bdot(s, v_h) for wi, qc, s, v_h in zip(wis, qcs, ss, vs)]
    dens = [wi * jnp.sum(q_h * n0, -1, keepdims=True) + jnp.sum(s, -1, keepdims=True)
            for wi, q_h, n0, s in zip(wis, qs, n0s, ss)]
    hhs = [num / jnp.maximum(jnp.abs(den), jnp.exp(-m)) for num, den, m in zip(nums, dens, ms)]
    m_cs = [m[C - 1:C, :] for m in ms]
    wends = [jnp.exp(fc_h[C - 1:C, :] - fc_h + li_all[:, h:h + 1] - m_c) for h, fc_h, m_c in zip(heads, fcs, m_cs)]
    dstates = [jnp.exp(fc_h[C - 1:C, :] + m0 - m_c) for fc_h, m0, m_c in zip(fcs, m0s, m_cs)]
    for h in heads:
        c_scr[h] = dstates[h] * c0s[h] + _bdot_tn(vs[h] * wends[h], ks[h])
        n_scr[h:h + 1, :] = dstates[h] * n0s[h] + jnp.sum(wends[h] * ks[h], 0, keepdims=True)
        m_new = jnp.where(lane == h, m_cs[h], m_new)
    outs = [hh * lax.rsqrt(jnp.mean(hh * hh, -1, keepdims=True) + NORM_EPS) for hh in hhs]
    m_scr[...] = m_new
    o_ref[...] = (jnp.concatenate(outs, axis=1) * normw[...] * _sigmoid(og_ref[...])).astype(o_ref.dtype)

    @pl.when(i == nblk - 1)
    def _():
        c_out[0] = c_scr[...]
        n_out[0] = n_scr[...]
        m_out[0] = m_new


def _mlstm_call(p, buf, seg, states, params):
    outs = _seq_call(
        _mlstm_body, seg,
        [(p, ML_QK, COL_ML_Q), (p, ML_QK, COL_ML_K), (p, ML_W, COL_ML_V), (p, ML_W, COL_ML_O), (p, LANE, COL_ML_IF)],
        list(states), params, buf, ML_W, RW_W + GD_W,
        [pltpu.VMEM((ML_HEADS, ML_DV, ML_DK), F32), pltpu.VMEM((ML_HEADS, ML_DK), F32), pltpu.VMEM((1, LANE), F32)],
        "mlstm_mix")
    return outs[0], tuple(outs[1:])


def _ffn_act_body(C, nblk, g_ref, u_ref, tail_in, convw, convb, _buf, o_ref, tail_out, tail):
    i = pl.program_id(1)

    @pl.when(i == 0)
    def _():
        tail[...] = tail_in[0]

    x = g_ref[...]
    t8 = tail[...]
    conv = x * convw[2:3, :] + convb[...]
    for s in (1, 2):
        conv = conv + _shift_rows(x, t8, s, C) * convw[2 - s:3 - s, :]
    tail[...] = x[C - TAIL:, :]
    o_ref[...] = (conv * _sigmoid(conv) * u_ref[...]).astype(o_ref.dtype)

    @pl.when(i == nblk - 1)
    def _():
        tail_out[0] = tail[...]


def _ffn_act_call(g, u, buf, seg, tail, params, in_row_base=None):
    outs = _seq_call(
        _ffn_act_body, seg, [(g, D_FF, 0), (u, D_FF, 0)], [tail], params, buf, D_FF, 0,
        [pltpu.VMEM((TAIL, D_FF), F32)], "ffn_act", in_row_base)
    return outs[0], outs[1]


def _ffn_up_body(tm, h_ref, wg_ref, wu_ref, convw, convb, o_ref, gt_ref, ut_ref, carry):
    i, j = pl.program_id(0), pl.program_id(1)
    h = h_ref[...]
    g = jnp.dot(h, wg_ref[...].astype(BF16), preferred_element_type=F32)
    u = jnp.dot(h, wu_ref[...].astype(BF16), preferred_element_type=F32)
    gt_ref[...] = g[tm - FFN_TAIL_ROWS:, :]
    ut_ref[...] = u[tm - FFN_TAIL_ROWS:, :]
    rows = i * tm + lax.broadcasted_iota(jnp.int32, (tm, 1), 0)
    g = jnp.where(rows >= ROW_META, g, 0.0)

    @pl.when(i == 0)
    def _():
        carry[j] = jnp.zeros((TAIL, COL_TILE), F32)

    t8 = carry[j]
    conv = g * convw[2:3, :] + convb[...]
    for s in (1, 2):
        conv = conv + _shift_rows(g, t8, s, tm) * convw[2 - s:3 - s, :]
    carry[j] = g[tm - TAIL:, :]
    o_ref[...] = (conv * _sigmoid(conv) * u).astype(o_ref.dtype)


def _ffn_up(hb, w_gate, w_up, layer, convw, convb, tm):
    m, k = hb.shape
    n = w_gate.shape[2]
    w_spec = pl.BlockSpec((None, k, COL_TILE), lambda i, j: (layer, 0, j))
    t_spec = pl.BlockSpec((FFN_TAIL_ROWS, COL_TILE), lambda i, j: (i, j))
    return pl.pallas_call(
        functools.partial(_ffn_up_body, tm), grid=(m // tm, n // COL_TILE),
        in_specs=[pl.BlockSpec((tm, k), lambda i, j: (i, 0), pipeline_mode=pl.Buffered(1)), w_spec, w_spec,
                  pl.BlockSpec((3, COL_TILE), lambda i, j: (0, j)), pl.BlockSpec((1, COL_TILE), lambda i, j: (0, j))],
        out_specs=[pl.BlockSpec((tm, COL_TILE), lambda i, j: (i, j)), t_spec, t_spec],
        out_shape=[jax.ShapeDtypeStruct((m, n), BF16), jax.ShapeDtypeStruct((m // tm * FFN_TAIL_ROWS, n), F32),
                   jax.ShapeDtypeStruct((m // tm * FFN_TAIL_ROWS, n), F32)],
        scratch_shapes=[pltpu.VMEM((n // COL_TILE, TAIL, COL_TILE), F32)],
        compiler_params=_cparams(2), name="ffn_up_act",
    )(hb, w_gate, w_up, convw, convb)


def _pad_cols(x, width):
    return jnp.pad(x, ((0, 0),) * (x.ndim - 1) + ((0, width - x.shape[-1]),))


def _lane_vec(x, start=0):
    return jnp.zeros((1, LANE), F32).at[0, start:start + x.shape[0]].set(x)


def _permute_cols(w):
    gd0, ml0 = RW_P, RW_P + GD_P
    gd_ab = w[..., gd0 + 3 * GD_W:gd0 + 3 * GD_W + 2 * GD_HEADS]
    ml_if = w[..., ml0 + 2 * ML_QK + ML_W:ml0 + 2 * ML_QK + ML_W + 2 * ML_HEADS]
    cols = [w[..., gd0:gd0 + 3 * GD_W], _pad_cols(gd_ab, LANE), _pad_cols(ml_if, LANE), w[..., :RW_P],
            w[..., ml0:ml0 + 2 * ML_QK], w[..., gd0 + 3 * GD_W + 2 * GD_HEADS:gd0 + GD_P],
            w[..., ml0 + 2 * ML_QK:ml0 + 2 * ML_QK + ML_W], w[..., ml0 + 2 * ML_QK + ML_W + 2 * ML_HEADS:]]
    return jnp.concatenate(cols, axis=-1)


def _pad_tail(buf, width):
    b, r, w = buf.shape
    return jnp.pad(buf, ((0, 0), (TAIL - r, 0), (0, width - w)))


def _layer(x, hb_norm_w, states, lp, wts, layer, segs):
    n = x.shape[0]
    hb = _rmsnorm(x, hb_norm_w, BF16, n, 0, 256)
    p = _matmul_wide(hb, wts["w_in"], layer, name="proj_in")
    mix = jnp.zeros((n, D_MODEL), BF16)
    new_states = []
    for seg, st in zip(segs, states):
        rw_st, gd_st, ml_st, ffn_tail = st
        if rw_st is None:
            rw_st, gd_st, ml_st, ffn_tail = new_states[-1][:4]
        mix, rw_new = _rwkv_call(p, mix, seg, rw_st, lp["rwkv"])
        mix, gd_new = _gdn_call(p, mix, seg, gd_st, lp["gdn"])
        mix, ml_new = _mlstm_call(p, mix, seg, ml_st, lp["mlstm"])
        new_states.append([rw_new, gd_new, ml_new, ffn_tail])
    x = _matmul(mix, wts["w_out"], layer, ROW_TILE, res=x, name="proj_out")
    hb = _rmsnorm(x, lp["norm_ffn_w"], BF16, n, 0, 256)
    act, g_tail, u_tail = _ffn_up(hb, wts["w_gate"], wts["w_up"], layer, lp["ffn"][0], lp["ffn"][1], ROW_TILE)
    tail_base = n - g_tail.shape[0]
    for idx, seg in enumerate(segs):
        if idx == 0 or states[idx][0] is None:
            end = seg[0] + seg[1] * seg[2] * seg[3]
            kept = end - TAIL >= n - FFN_TAIL_ROWS
            new_states[idx][3] = g_tail[end - TAIL - tail_base:end - tail_base][None] if kept else None
        else:
            assert seg[0] >= n - FFN_TAIL_ROWS
            act, new_states[idx][3] = _ffn_act_call(g_tail, u_tail, act, seg, new_states[idx][3], lp["ffn"],
                                                    in_row_base=seg[0] - tail_base)
    x = _matmul(act, wts["w_down"], layer, ROW_TILE // 2, res=x, name="ffn_down")
    return x, new_states


def kernel(x_prompt, x_sample, state_rwkv_wkv, state_rwkv_shift, state_gdn, cache_gdn_conv, state_mlstm_c, state_mlstm_n, state_mlstm_m, cache_ffn_conv, meta_tokens, norm_mix_w, w_in, rwkv_mu, rwkv_w0, rwkv_w_up, rwkv_a0, rwkv_a_up, rwkv_g_up, rwkv_k_k, rwkv_k_a, rwkv_r_k, rwkv_ln_w, rwkv_ln_b, gdn_conv_w, gdn_a_log, gdn_dt_bias, gdn_norm_w, mlstm_i_b, mlstm_f_b, mlstm_norm_w, w_out, norm_ffn_w, ffn_w_gate, ffn_w_up, ffn_conv_w, ffn_conv_b, ffn_w_down, final_norm_w):
    depth = w_in.shape[0]
    seq = x_prompt.shape[1]
    dec_b, dec_seq = x_sample.shape[0], x_sample.shape[1]
    row_sample = ROW_PROMPT + seq
    n_rows = row_sample + dec_b * dec_seq
    n_pad = -(-n_rows // ROW_TILE) * ROW_TILE
    segs = ((ROW_META, N_META, 1, 1), (ROW_PROMPT, CHUNK, 1, seq // CHUNK), (row_sample, dec_seq, dec_b, 1))

    x = jnp.concatenate([
        jnp.zeros((ROW_META, D_MODEL), F32), meta_tokens.astype(F32), x_prompt[0],
        x_sample.reshape(dec_b * dec_seq, D_MODEL), jnp.zeros((n_pad - n_rows, D_MODEL), F32)], axis=0)

    row = lambda a: a.reshape(1, -1)
    zero_states = ((jnp.zeros((1, 1, RW_P), F32), jnp.zeros((1, RW_HEADS, RW_HD, RW_HD), F32)),
                   (jnp.zeros((1, TAIL, 3 * GD_W), F32), jnp.zeros((1, GD_HEADS, GD_HD, GD_HD), F32)),
                   (jnp.zeros((1, ML_HEADS, ML_DV, ML_DK), F32), jnp.zeros((1, ML_HEADS, ML_DK), F32),
                    jnp.zeros((1, 1, LANE), F32)),
                   jnp.zeros((1, TAIL, D_FF), F32))
    chained = (None, None, None, None)

    wts = {
        "w_in": _permute_cols(w_in).astype(BF16),
        "w_out": w_out,
        "w_gate": ffn_w_gate,
        "w_up": ffn_w_up,
        "w_down": ffn_w_down.astype(BF16),
    }

    p_out, s_out = [], []
    for l in range(depth):
        lp = {
            "rwkv": [row(rwkv_mu[l]), row(rwkv_w0[l]), rwkv_w_up[l], row(rwkv_a0[l]), rwkv_a_up[l], rwkv_g_up[l],
                     row(rwkv_k_k[l]), row(rwkv_k_a[l]), row(rwkv_r_k[l]), row(rwkv_ln_w[l]), row(rwkv_ln_b[l])],
            "gdn": [gdn_conv_w[l], _lane_vec(gdn_a_log[l]), _lane_vec(gdn_dt_bias[l]), row(gdn_norm_w[l])],
            "mlstm": [_lane_vec(mlstm_i_b[l]), _lane_vec(mlstm_f_b[l], ML_HEADS), row(mlstm_norm_w[l])],
            "norm_ffn_w": norm_ffn_w[l],
            "ffn": [ffn_conv_w[l], row(ffn_conv_b[l])],
        }
        sample_states = ((state_rwkv_shift[l][:, None, :], state_rwkv_wkv[l]),
                         (_pad_tail(cache_gdn_conv[l], 3 * GD_W), state_gdn[l]),
                         (state_mlstm_c[l], state_mlstm_n[l], _pad_cols(state_mlstm_m[l], LANE)[:, None, :]),
                         _pad_tail(cache_ffn_conv[l], D_FF))
        x, st = _layer(x, norm_mix_w[l], (zero_states, chained, sample_states), lp, wts, l, segs)
        p_out.append(st[1])
        s_out.append(st[2])

    y_prompt = _rmsnorm(x, final_norm_w, F32, seq, ROW_PROMPT, CHUNK).reshape(1, seq, D_MODEL)
    y_sample = _rmsnorm(x, final_norm_w, F32, dec_b * dec_seq, row_sample, CHUNK).reshape(dec_b, dec_seq, D_MODEL)

    def collect(sts):
        stack = lambda f: jnp.stack([f(st) for st in sts], 0)
        return (stack(lambda st: st[0][1]), stack(lambda st: st[0][0][:, 0, :]),
                stack(lambda st: st[1][1]), stack(lambda st: st[1][0][:, TAIL - 3:, :]),
                stack(lambda st: st[2][0]), stack(lambda st: st[2][1]), stack(lambda st: st[2][2][:, 0, :ML_HEADS]),
                stack(lambda st: st[3][:, TAIL - 2:, :D_FF]))

    return (y_prompt, y_sample) + collect(p_out) + collect(s_out)
```

```python
import functools
import math

import jax
import jax.numpy as jnp
from jax import lax
from jax.experimental import pallas as pl
from jax.experimental.pallas import tpu as pltpu

F32 = jnp.float32
BF16 = jnp.bfloat16

D_MODEL = 4096
N_META = 16
CHUNK = 64
NORM_EPS = 1e-6
RW_HEADS, RW_HD = 24, 64
RW_W = RW_HEADS * RW_HD
RW_P = 3 * RW_W + 64 + 64 + 128
RW_GN_EPS = 64e-5
GD_HEADS, GD_HD = 12, 128
GD_W = GD_HEADS * GD_HD
GD_P = 4 * GD_W + 2 * GD_HEADS
ML_HEADS, ML_DK, ML_DV = 4, 128, 256
ML_W = ML_HEADS * ML_DV
ML_QK = ML_HEADS * ML_DK
ML_P = 2 * ML_QK + 2 * ML_W + 2 * ML_HEADS
D_FF = 11008

LANE = 128
TAIL = 8
ROW_META = 48
ROW_PROMPT = 64
ROW_TILE = 2176
COL_TILE = 256
FFN_TAIL_ROWS = 512
VMEM_LIMIT = 56 * 1024 * 1024

P_MAIN = RW_P + 3 * GD_W
T_AB = 0
T_IF = 128
T_MLQ = 256
T_MLK = 768
T_Z = 1280
T_MLV = 2816
T_MLO = 3840
P_TAIL = 4864


def _cparams(n_axes):
    return pltpu.CompilerParams(dimension_semantics=("arbitrary",) * n_axes, vmem_limit_bytes=VMEM_LIMIT)


def _bdot(a, b):
    return jnp.dot(a.astype(BF16), b.astype(BF16), preferred_element_type=F32)


def _bdot_nt(a, b):
    return lax.dot_general(a.astype(BF16), b.astype(BF16), (((1,), (1,)), ((), ())), preferred_element_type=F32)


def _bdot_tn(a, b):
    return lax.dot_general(a.astype(BF16), b.astype(BF16), (((0,), (0,)), ((), ())), preferred_element_type=F32)


def _split3(x):
    hi = x.astype(BF16)
    r1 = x - hi.astype(F32)
    mid = r1.astype(BF16)
    lo = (r1 - mid.astype(F32)).astype(BF16)
    return hi, mid, lo


def _cumsum_rows(x, tril_bf):
    hi, mid, lo = _split3(x)
    d = lambda t: jnp.dot(tril_bf, t, preferred_element_type=F32)
    return d(hi) + d(mid) + d(lo)


def _cols_to_rows(x, n_rows):
    sel = (lax.broadcasted_iota(jnp.int32, (n_rows, LANE), 0) == lax.broadcasted_iota(jnp.int32, (n_rows, LANE), 1)).astype(BF16)
    hi, mid, lo = _split3(x)
    d = lambda t: lax.dot_general(sel, t, (((1,), (1,)), ((), ())), preferred_element_type=F32)
    return d(hi) + d(mid) + d(lo)


def _head_selectors(width, hd):
    sh = int(math.log2(hd))
    e = (lax.broadcasted_iota(jnp.int32, (width, LANE), 0) >> sh) == lax.broadcasted_iota(jnp.int32, (width, LANE), 1)
    et = lax.broadcasted_iota(jnp.int32, (LANE, width), 0) == (lax.broadcasted_iota(jnp.int32, (LANE, width), 1) >> sh)
    return e.astype(BF16), et.astype(BF16)


def _head_sums(x, sel):
    e, et = sel
    hi, lo, _ = _split3(x)
    d = lambda t: jnp.dot(t, e, preferred_element_type=F32)
    hi, lo, _ = _split3(d(hi) + d(lo))
    d = lambda t: jnp.dot(t, et, preferred_element_type=F32)
    return d(hi) + d(lo)


def _tri_masks(C):
    ri = lax.broadcasted_iota(jnp.int32, (C, C), 0)
    ci = lax.broadcasted_iota(jnp.int32, (C, C), 1)
    return ri >= ci, ri > ci, ri == ci


def _unit_lower_inverse(As, eye, C):
    Ps = [-A for A in As]
    Ts = [jnp.where(eye, 1.0, P) for P in Ps]
    for _ in range(int(math.log2(C)) - 1):
        Ps = [_bdot(P, P) for P in Ps]
        Ts = [T + _bdot(T, P) for T, P in zip(Ts, Ps)]
        yield
    return Ts


def _interleave(gens):
    results, alive = [None] * len(gens), list(range(len(gens)))
    while alive:
        for idx in list(alive):
            try:
                next(gens[idx])
            except StopIteration as stop:
                results[idx] = stop.value
                alive.remove(idx)
    return results


def _sigmoid(x):
    return jax.nn.sigmoid(x)


def _softplus(x):
    return jnp.maximum(x, 0.0) + jnp.log(1.0 + jnp.exp(-jnp.abs(x)))


def _shift_rows(x, tail, s, C):
    xr = pltpu.roll(x, s, 0)
    r8 = lax.broadcasted_iota(jnp.int32, (TAIL, 1), 0)
    top = jnp.where(r8 < s, pltpu.roll(tail, s, 0), xr[0:TAIL])
    return jnp.concatenate([top, xr[TAIL:]], axis=0)


def _rmsnorm_body(x_ref, w_ref, o_ref):
    x = x_ref[...]
    y = x * lax.rsqrt(jnp.mean(x * x, -1, keepdims=True) + NORM_EPS)
    o_ref[...] = (y * w_ref[...]).astype(o_ref.dtype)


def _rmsnorm(x, w, out_dtype, rows, row_base, block):
    d = x.shape[1]
    rb = row_base // block
    return pl.pallas_call(
        _rmsnorm_body,
        grid=(rows // block,),
        in_specs=[pl.BlockSpec((block, d), lambda i: (rb + i, 0)), pl.BlockSpec((1, d), lambda i: (0, 0))],
        out_specs=pl.BlockSpec((block, d), lambda i: (i, 0)),
        out_shape=jax.ShapeDtypeStruct((rows, d), out_dtype),
        compiler_params=_cparams(1),
        name="rmsnorm",
    )(x, w.reshape(1, d))


def _mm_body(a_ref, w_ref, o_ref):
    o_ref[...] = jnp.dot(a_ref[...], w_ref[...].astype(BF16), preferred_element_type=F32)


def _mm_res_body(a_ref, w_ref, r_ref, o_ref):
    o_ref[...] = r_ref[...] + jnp.dot(a_ref[...], w_ref[...].astype(BF16), preferred_element_type=F32)


def _matmul(a, w, layer, tm, res=None, n_cols=None, name="matmul"):
    m, k = a.shape
    n = w.shape[2] if n_cols is None else n_cols
    a_spec = pl.BlockSpec((tm, k), lambda i, j: (i, 0), pipeline_mode=pl.Buffered(1))
    w_spec = pl.BlockSpec((None, k, COL_TILE), lambda i, j: (layer, 0, j))
    o_spec = pl.BlockSpec((tm, COL_TILE), lambda i, j: (i, j))
    if res is None:
        body, specs, args = _mm_body, [a_spec, w_spec], (a, w)
    else:
        body, specs, args = _mm_res_body, [a_spec, w_spec, o_spec], (a, w, res)
    return pl.pallas_call(
        body, grid=(m // tm, n // COL_TILE), in_specs=specs, out_specs=o_spec,
        out_shape=jax.ShapeDtypeStruct((m, n), F32),
        compiler_params=_cparams(2), name=name,
    )(*args)


def _row_spec(width, col_start, seg):
    row_base, C, B, nblk = seg
    rb, cb = row_base // C, col_start // width
    return pl.BlockSpec((C, width), lambda b, i: (rb + b * nblk + i, cb))


def _state_spec(shape):
    nd = len(shape)
    return pl.BlockSpec((1,) + tuple(shape[1:]), lambda b, i: (b,) + (0,) * (nd - 1))


def _param_spec(shape):
    nd = len(shape)
    return pl.BlockSpec(tuple(shape), lambda b, i: (0,) * nd)


def _seq_call(body, seg, row_ins, state_ins, params, buf, out_width, out_col, scratch, name, in_row_base=None):
    row_base, C, B, nblk = seg
    in_seg = seg if in_row_base is None else (in_row_base, C, B, nblk)
    in_specs = [_row_spec(w, c, in_seg) for (_, w, c) in row_ins]
    in_specs += [_state_spec(s.shape) for s in state_ins]
    in_specs += [_param_spec(p.shape) for p in params]
    in_specs += [pl.BlockSpec(memory_space=pl.ANY)]
    args = [a for (a, _, _) in row_ins] + list(state_ins) + list(params) + [buf]
    out_specs = [_row_spec(out_width, out_col, seg)] + [_state_spec(s.shape) for s in state_ins]
    out_shape = [jax.ShapeDtypeStruct(buf.shape, buf.dtype)] + [jax.ShapeDtypeStruct(s.shape, s.dtype) for s in state_ins]
    return pl.pallas_call(
        functools.partial(body, C, nblk), grid=(B, nblk), in_specs=in_specs, out_specs=out_specs,
        out_shape=out_shape, scratch_shapes=scratch,
        input_output_aliases={len(args) - 1: 0},
        compiler_params=_cparams(2), name=name,
    )(*args)


def _rwkv_main(C, p, carry, s_scr, prm):
    mu, w0, wup, a0, aup, gup, kkw, kaw, rkw, lnw, lnb = prm
    rows = lax.broadcasted_iota(jnp.int32, (C, 1), 0)
    prev = jnp.where(rows == 0, carry[...], pltpu.roll(p, 1, 0))
    carry[...] = p[C - 1:C, :]
    xs = p + (prev - p) * mu[...]
    r, k, v = xs[:, 0:RW_W], xs[:, RW_W:2 * RW_W], xs[:, 2 * RW_W:3 * RW_W]
    dw, da, dg = xs[:, 3 * RW_W:3 * RW_W + 64], xs[:, 3 * RW_W + 64:3 * RW_W + 128], xs[:, 3 * RW_W + 128:RW_P]
    lw = -math.exp(-0.5) * _sigmoid(w0[...] + _bdot(jnp.tanh(dw), wup[...]))
    a = _sigmoid(a0[...] + _bdot(da, aup[...]))
    g = _bdot(_sigmoid(dg), gup[...])
    sel = _head_selectors(RW_W, RW_HD)
    kkx = k * kkw[...]
    kkn = kkx * lax.rsqrt(_head_sums(kkx * kkx, sel) + 1e-6)
    kt = k * (1.0 + (a - 1.0) * kaw[...])
    bonus = _head_sums(r * kt * rkw[...], sel) * v
    kb = kkn * a
    yield

    tril, strict, eye = _tri_masks(C)
    cl = _cumsum_rows(lw, tril.astype(BF16))
    w_in = jnp.exp(cl)
    w_ex = jnp.exp(cl - lw)
    w_inv = jnp.exp(-cl)
    w_end = jnp.exp(cl[C - 1:C, :] - cl)

    heads = range(RW_HEADS)
    sls = [slice(RW_HD * h, RW_HD * (h + 1)) for h in heads]
    kk_ex, r_in = kkn * w_ex, r * w_in
    kt_inv, kb_inv = kt * w_inv, kb * w_inv
    kt_end, kb_end = kt * w_end, kb * w_end
    yield
    lhss = [jnp.concatenate([kk_ex[:, sl], r_in[:, sl]], axis=0) for sl in sls]
    rhss = [jnp.concatenate([kt_inv[:, sl], kb_inv[:, sl]], axis=0) for sl in sls]
    scs = [_bdot_nt(lhs, rhs) for lhs, rhs in zip(lhss, rhss)]
    yield
    s0s = [s_scr[h] for h in heads]
    pss = [_bdot_nt(lhs, s0) for lhs, s0 in zip(lhss, s0s)]
    yield
    negs = [-(ps[:C] + _bdot(jnp.where(strict, sc[:C, :C], 0.0), v[:, sl])) for ps, sc, sl in zip(pss, scs, sls)]
    yield
    invs = yield from _unit_lower_inverse([jnp.where(strict, sc[:C, C:], 0.0) for sc in scs], eye, C)
    us = [_bdot(t, n) for t, n in zip(invs, negs)]
    yield
    vus = [jnp.concatenate([v[:, sl], u], axis=0) for u, sl in zip(us, sls)]
    a_rs = [jnp.concatenate([jnp.where(tril, sc[C:, :C], 0.0), jnp.where(tril, sc[C:, C:], 0.0)], axis=1) for sc in scs]
    ys = [ps[C:] + _bdot(a_r, vu) for ps, a_r, vu in zip(pss, a_rs, vus)]
    yield
    kes = [jnp.concatenate([kt_end[:, sl], kb_end[:, sl]], axis=0) for sl in sls]
    for h in heads:
        s_scr[h] = s0s[h] * w_in[C - 1:C, sls[h]] + _bdot_tn(vus[h], kes[h])
    yield
    y = jnp.concatenate(ys, axis=1)
    yc = y - _head_sums(y, sel) * (1.0 / RW_HD)
    yn = yc * lax.rsqrt(_head_sums(yc * yc, sel) * (1.0 / RW_HD) + RW_GN_EPS)
    return (yn * lnw[...] + lnb[...] + bonus) * g


def _gdn_main(C, x, z, ab, tail, s_scr, prm):
    convw, alog, dtb, normw = prm
    t8 = tail[...]
    conv = x * convw[3:4, :]
    for s in (1, 2, 3):
        conv = conv + _shift_rows(x, t8, s, C) * convw[3 - s:4 - s, :]
    tail[...] = x[C - TAIL:, :]
    act = conv * _sigmoid(conv)
    q, k, v = act[:, 0:GD_W], act[:, GD_W:2 * GD_W], act[:, 2 * GD_W:3 * GD_W]
    yield

    g_all = -jnp.exp(alog[...]) * _softplus(ab + dtb[...])
    beta_all = _sigmoid(ab)
    tril, strict, eye = _tri_masks(C)
    gc = _cumsum_rows(g_all, tril.astype(BF16))
    gr = _cols_to_rows(gc, 16)
    eg = jnp.exp(gc)
    yield

    heads = range(GD_HEADS)
    sls = [slice(GD_HD * h, GD_HD * (h + 1)) for h in heads]
    qs = [q[:, sl] for sl in sls]
    qs = [t * (lax.rsqrt(jnp.sum(t * t, -1, keepdims=True) + 1e-6) * GD_HD ** -0.5) for t in qs]
    ks = [k[:, sl] for sl in sls]
    ks = [t * lax.rsqrt(jnp.sum(t * t, -1, keepdims=True) + 1e-6) for t in ks]
    yield
    betas = [beta_all[:, GD_HEADS + h:GD_HEADS + h + 1] for h in heads]
    decays = [jnp.where(tril, jnp.exp(jnp.where(tril, gc[:, h:h + 1] - gr[h:h + 1, :], 0.0)), 0.0) for h in heads]
    yield
    scs = [_bdot_nt(jnp.concatenate([k_h, q_h], axis=0), k_h) for k_h, q_h in zip(ks, qs)]
    yield
    invs = yield from _unit_lower_inverse(
        [jnp.where(strict, beta * sc[:C] * decay, 0.0) for beta, sc, decay in zip(betas, scs, decays)], eye, C)
    rhss = [jnp.concatenate([v[:, sl] * beta, k_h * (beta * eg[:, h:h + 1])], axis=1)
            for h, sl, beta, k_h in zip(heads, sls, betas, ks)]
    sols = [_bdot(t, rhs) for t, rhs in zip(invs, rhss)]
    yield
    s0s = [s_scr[h] for h in heads]
    pss = [_bdot_nt(jnp.concatenate([sol[:, GD_HD:], q_h], axis=0), s0) for sol, q_h, s0 in zip(sols, qs, s0s)]
    yield
    us = [sol[:, :GD_HD] - ps[:C] for sol, ps in zip(sols, pss)]
    os_ = [eg[:, h:h + 1] * ps[C:] + _bdot(sc[C:] * decay, u) for h, ps, sc, decay, u in zip(heads, pss, scs, decays, us)]
    yield
    for h in heads:
        gl = gc[C - 1:C, h:h + 1]
        s_scr[h] = jnp.exp(gl) * s0s[h] + _bdot_tn(us[h] * jnp.exp(gl - gc[:, h:h + 1]), ks[h])
    yield
    outs = [o * lax.rsqrt(jnp.mean(o * o, -1, keepdims=True) + NORM_EPS) * normw[...] for o in os_]
    return jnp.concatenate(outs, axis=1) * (z * _sigmoid(z))


def _mlstm_main(C, q, k, v, og, gates, c_scr, n_scr, m_scr, prm):
    ib, fb, normw = prm
    li_all = gates + ib[...]
    x = gates + fb[...]
    lf_all = jnp.minimum(x, 0.0) - jnp.log(1.0 + jnp.exp(-jnp.abs(x)))
    tril, _, _ = _tri_masks(C)
    fc = _cumsum_rows(lf_all, tril.astype(BF16))
    fr = _cols_to_rows(fc, 8)
    lir = _cols_to_rows(li_all, 8)
    lane = lax.broadcasted_iota(jnp.int32, (1, LANE), 1)
    m_all = m_scr[...]
    m_new = m_all
    yield

    heads = range(ML_HEADS)
    qs = [q[:, ML_DK * h:ML_DK * (h + 1)] * ML_DK ** -0.5 for h in heads]
    ks = [k[:, ML_DK * h:ML_DK * (h + 1)] for h in heads]
    vs = [v[:, ML_DV * h:ML_DV * (h + 1)] for h in heads]
    c0s = [c_scr[h] for h in heads]
    n0s = [n_scr[h:h + 1, :] for h in heads]
    qks = [_bdot_nt(q_h, k_h) for q_h, k_h in zip(qs, ks)]
    qcs = [_bdot_nt(q_h, c0) for q_h, c0 in zip(qs, c0s)]
    yield
    fcs = [fc[:, ML_HEADS + h:ML_HEADS + h + 1] for h in heads]
    m0s = [m_all[:, h:h + 1] for h in heads]
    dmats = [jnp.where(tril, fc_h - fr[ML_HEADS + h:ML_HEADS + h + 1, :] + lir[h:h + 1, :], -jnp.inf)
             for h, fc_h in zip(heads, fcs)]
    inters = [fc_h + m0 for fc_h, m0 in zip(fcs, m0s)]
    ms = [jnp.maximum(inter, jnp.max(dmat, -1, keepdims=True)) for inter, dmat in zip(inters, dmats)]
    yield
    ss = [qk * jnp.exp(dmat - m) for qk, dmat, m in zip(qks, dmats, ms)]
    wis = [jnp.exp(inter - m) for inter, m in zip(inters, ms)]
    yield
    nums = [wi * qc + _bdot(s, v_h) for wi, qc, s, v_h in zip(wis, qcs, ss, vs)]
    dens = [wi * jnp.sum(q_h * n0, -1, keepdims=True) + jnp.sum(s, -1, keepdims=True)
            for wi, q_h, n0, s in zip(wis, qs, n0s, ss)]
    yield
    hhs = [num / jnp.maximum(jnp.abs(den), jnp.exp(-m)) for num, den, m in zip(nums, dens, ms)]
    m_cs = [m[C - 1:C, :] for m in ms]
    wends = [jnp.exp(fc_h[C - 1:C, :] - fc_h + li_all[:, h:h + 1] - m_c) for h, fc_h, m_c in zip(heads, fcs, m_cs)]
    dstates = [jnp.exp(fc_h[C - 1:C, :] + m0 - m_c) for fc_h, m0, m_c in zip(fcs, m0s, m_cs)]
    for h in heads:
        c_scr[h] = dstates[h] * c0s[h] + _bdot_tn(vs[h] * wends[h], ks[h])
        n_scr[h:h + 1, :] = dstates[h] * n0s[h] + jnp.sum(wends[h] * ks[h], 0, keepdims=True)
        m_new = jnp.where(lane == h, m_cs[h], m_new)
    m_scr[...] = m_new
    yield
    outs = [hh * lax.rsqrt(jnp.mean(hh * hh, -1, keepdims=True) + NORM_EPS) for hh in hhs]
    return jnp.concatenate(outs, axis=1) * normw[...] * _sigmoid(og)


N_MIX_STATES = 7
N_RW_PRM, N_GD_PRM, N_ML_PRM = 11, 4, 3


def _mixer_body(C, nblk, pm_ref, pt_ref, *refs):
    ins, refs = refs[:N_MIX_STATES], refs[N_MIX_STATES:]
    rw_prm, refs = refs[:N_RW_PRM], refs[N_RW_PRM:]
    gd_prm, refs = refs[:N_GD_PRM], refs[N_GD_PRM:]
    ml_prm, refs = refs[:N_ML_PRM], refs[N_ML_PRM:]
    o_ref, refs = refs[1], refs[2:]
    outs, scr = refs[:N_MIX_STATES], refs[N_MIX_STATES:]
    carry, s_rw, tail, s_gd, c_scr, n_scr, m_scr = scr
    i = pl.program_id(1)

    @pl.when(i == 0)
    def _():
        for dst, src in zip(scr, ins):
            dst[...] = src[0]

    pm, pt = pm_ref[...], pt_ref[...]
    o_rw, o_gd, o_ml = _interleave([
        _rwkv_main(C, pm[:, 0:RW_P], carry, s_rw, rw_prm),
        _gdn_main(C, pm[:, RW_P:P_MAIN], pt[:, T_Z:T_Z + GD_W], pt[:, T_AB:T_AB + LANE], tail, s_gd, gd_prm),
        _mlstm_main(C, pt[:, T_MLQ:T_MLQ + ML_QK], pt[:, T_MLK:T_MLK + ML_QK], pt[:, T_MLV:T_MLV + ML_W],
                    pt[:, T_MLO:T_MLO + ML_W], pt[:, T_IF:T_IF + LANE], c_scr, n_scr, m_scr, ml_prm)])
    o_ref[:, 0:RW_W] = o_rw.astype(o_ref.dtype)
    o_ref[:, RW_W:RW_W + GD_W] = o_gd.astype(o_ref.dtype)
    o_ref[:, RW_W + GD_W:] = o_ml.astype(o_ref.dtype)

    @pl.when(i == nblk - 1)
    def _():
        outs[0][0] = pm[C - 1:C, 0:RW_P]
        for dst, src in zip(outs[1:], scr[1:]):
            dst[0] = src[...]


def _mixer_call(p_main, p_tail, buf, seg, states, params):
    scratch = [pltpu.VMEM((1, RW_P), F32), pltpu.VMEM((RW_HEADS, RW_HD, RW_HD), F32),
               pltpu.VMEM((TAIL, 3 * GD_W), F32), pltpu.VMEM((GD_HEADS, GD_HD, GD_HD), F32),
               pltpu.VMEM((ML_HEADS, ML_DV, ML_DK), F32), pltpu.VMEM((ML_HEADS, ML_DK), F32), pltpu.VMEM((1, LANE), F32)]
    outs = _seq_call(_mixer_body, seg, [(p_main, P_MAIN, 0), (p_tail, P_TAIL, 0)], list(states), params,
                     buf, D_MODEL, 0, scratch, "mixers")
    return outs[0], list(outs[1:])


def _ffn_act_body(C, nblk, g_ref, u_ref, tail_in, convw, convb, _buf, o_ref, tail_out, tail):
    i = pl.program_id(1)

    @pl.when(i == 0)
    def _():
        tail[...] = tail_in[0]

    x = g_ref[...]
    t8 = tail[...]
    conv = x * convw[2:3, :] + convb[...]
    for s in (1, 2):
        conv = conv + _shift_rows(x, t8, s, C) * convw[2 - s:3 - s, :]
    tail[...] = x[C - TAIL:, :]
    o_ref[...] = (conv * _sigmoid(conv) * u_ref[...]).astype(o_ref.dtype)

    @pl.when(i == nblk - 1)
    def _():
        tail_out[0] = tail[...]


def _ffn_act_call(g, u, buf, seg, tail, params, in_row_base=None):
    outs = _seq_call(
        _ffn_act_body, seg, [(g, D_FF, 0), (u, D_FF, 0)], [tail], params, buf, D_FF, 0,
        [pltpu.VMEM((TAIL, D_FF), F32)], "ffn_act", in_row_base)
    return outs[0], outs[1]


def _ffn_up_body(tm, h_ref, wg_ref, wu_ref, convw, convb, o_ref, gt_ref, ut_ref, carry):
    i, j = pl.program_id(0), pl.program_id(1)
    h = h_ref[...]
    g = jnp.dot(h, wg_ref[...].astype(BF16), preferred_element_type=F32)
    u = jnp.dot(h, wu_ref[...].astype(BF16), preferred_element_type=F32)
    gt_ref[...] = g[tm - FFN_TAIL_ROWS:, :]
    ut_ref[...] = u[tm - FFN_TAIL_ROWS:, :]
    rows = i * tm + lax.broadcasted_iota(jnp.int32, (tm, 1), 0)
    g = jnp.where(rows >= ROW_META, g, 0.0)

    @pl.when(i == 0)
    def _():
        carry[j] = jnp.zeros((TAIL, COL_TILE), F32)

    t8 = carry[j]
    conv = g * convw[2:3, :] + convb[...]
    for s in (1, 2):
        conv = conv + _shift_rows(g, t8, s, tm) * convw[2 - s:3 - s, :]
    carry[j] = g[tm - TAIL:, :]
    o_ref[...] = (conv * _sigmoid(conv) * u).astype(o_ref.dtype)


def _ffn_up(hb, w_gate, w_up, layer, convw, convb, tm):
    m, k = hb.shape
    n = w_gate.shape[2]
    w_spec = pl.BlockSpec((None, k, COL_TILE), lambda i, j: (layer, 0, j))
    t_spec = pl.BlockSpec((FFN_TAIL_ROWS, COL_TILE), lambda i, j: (i, j))
    return pl.pallas_call(
        functools.partial(_ffn_up_body, tm), grid=(m // tm, n // COL_TILE),
        in_specs=[pl.BlockSpec((tm, k), lambda i, j: (i, 0), pipeline_mode=pl.Buffered(1)), w_spec, w_spec,
                  pl.BlockSpec((3, COL_TILE), lambda i, j: (0, j)), pl.BlockSpec((1, COL_TILE), lambda i, j: (0, j))],
        out_specs=[pl.BlockSpec((tm, COL_TILE), lambda i, j: (i, j)), t_spec, t_spec],
        out_shape=[jax.ShapeDtypeStruct((m, n), BF16), jax.ShapeDtypeStruct((m // tm * FFN_TAIL_ROWS, n), F32),
                   jax.ShapeDtypeStruct((m // tm * FFN_TAIL_ROWS, n), F32)],
        scratch_shapes=[pltpu.VMEM((n // COL_TILE, TAIL, COL_TILE), F32)],
        compiler_params=_cparams(2), name="ffn_up_act",
    )(hb, w_gate, w_up, convw, convb)


def _pad_cols(x, width):
    return jnp.pad(x, ((0, 0),) * (x.ndim - 1) + ((0, width - x.shape[-1]),))


def _lane_vec(x, start=0):
    return jnp.zeros((1, LANE), F32).at[0, start:start + x.shape[0]].set(x)


def _tail_cols(w):
    ml0 = RW_P + GD_P
    gd_ab = w[..., P_MAIN:P_MAIN + 2 * GD_HEADS]
    ml_if = w[..., ml0 + 2 * ML_QK + ML_W:ml0 + 2 * ML_QK + ML_W + 2 * ML_HEADS]
    cols = [_pad_cols(gd_ab, LANE), _pad_cols(ml_if, LANE), w[..., ml0:ml0 + 2 * ML_QK],
            w[..., P_MAIN + 2 * GD_HEADS:ml0], w[..., ml0 + 2 * ML_QK:ml0 + 2 * ML_QK + ML_W],
            w[..., ml0 + 2 * ML_QK + ML_W + 2 * ML_HEADS:ml0 + ML_P]]
    return jnp.concatenate(cols, axis=-1)


def _pad_tail(buf, width):
    b, r, w = buf.shape
    return jnp.pad(buf, ((0, 0), (TAIL - r, 0), (0, width - w)))


def _layer(x, states, lp, wts, layer, segs):
    n = x.shape[0]
    hb = _rmsnorm(x, lp["norm_mix_w"], BF16, n, 0, 256)
    p_main = _matmul(hb, wts["w_in"], layer, ROW_TILE, n_cols=P_MAIN, name="proj_in_main")
    p_tail = _matmul(hb, wts["w_in_tail"], layer, ROW_TILE, name="proj_in_tail")
    mix = jnp.zeros((n, D_MODEL), BF16)
    new_states = []
    for seg, (mix_st, ffn_tail) in zip(segs, states):
        if mix_st is None:
            mix_st = new_states[-1][0]
        mix, mix_new = _mixer_call(p_main, p_tail, mix, seg, mix_st, lp["mixers"])
        new_states.append([mix_new, ffn_tail])
    x = _matmul(mix, wts["w_out"], layer, ROW_TILE, res=x, name="proj_out")
    hb = _rmsnorm(x, lp["norm_ffn_w"], BF16, n, 0, 256)
    act, g_tail, u_tail = _ffn_up(hb, wts["w_gate"], wts["w_up"], layer, lp["ffn"][0], lp["ffn"][1], ROW_TILE)
    tail_base = n - g_tail.shape[0]
    for idx, seg in enumerate(segs):
        if idx == 0 or states[idx][0] is None:
            end = seg[0] + seg[1] * seg[2] * seg[3]
            kept = end - TAIL >= n - FFN_TAIL_ROWS
            new_states[idx][1] = g_tail[end - TAIL - tail_base:end - tail_base][None] if kept else None
        else:
            assert seg[0] >= n - FFN_TAIL_ROWS
            act, new_states[idx][1] = _ffn_act_call(g_tail, u_tail, act, seg, new_states[idx][1], lp["ffn"],
                                                    in_row_base=seg[0] - tail_base)
    x = _matmul(act, wts["w_down"], layer, ROW_TILE // 2, res=x, name="ffn_down")
    return x, new_states


def kernel(x_prompt, x_sample, state_rwkv_wkv, state_rwkv_shift, state_gdn, cache_gdn_conv, state_mlstm_c, state_mlstm_n, state_mlstm_m, cache_ffn_conv, meta_tokens, norm_mix_w, w_in, rwkv_mu, rwkv_w0, rwkv_w_up, rwkv_a0, rwkv_a_up, rwkv_g_up, rwkv_k_k, rwkv_k_a, rwkv_r_k, rwkv_ln_w, rwkv_ln_b, gdn_conv_w, gdn_a_log, gdn_dt_bias, gdn_norm_w, mlstm_i_b, mlstm_f_b, mlstm_norm_w, w_out, norm_ffn_w, ffn_w_gate, ffn_w_up, ffn_conv_w, ffn_conv_b, ffn_w_down, final_norm_w):
    depth = w_in.shape[0]
    seq = x_prompt.shape[1]
    dec_b, dec_seq = x_sample.shape[0], x_sample.shape[1]
    row_sample = ROW_PROMPT + seq
    n_rows = row_sample + dec_b * dec_seq
    n_pad = -(-n_rows // ROW_TILE) * ROW_TILE
    segs = ((ROW_META, N_META, 1, 1), (ROW_PROMPT, CHUNK, 1, seq // CHUNK), (row_sample, dec_seq, dec_b, 1))

    x = jnp.concatenate([
        jnp.zeros((ROW_META, D_MODEL), F32), meta_tokens.astype(F32), x_prompt[0],
        x_sample.reshape(dec_b * dec_seq, D_MODEL), jnp.zeros((n_pad - n_rows, D_MODEL), F32)], axis=0)

    row = lambda a: a.reshape(1, -1)
    zero_mix = [jnp.zeros((1, 1, RW_P), F32), jnp.zeros((1, RW_HEADS, RW_HD, RW_HD), F32),
                jnp.zeros((1, TAIL, 3 * GD_W), F32), jnp.zeros((1, GD_HEADS, GD_HD, GD_HD), F32),
                jnp.zeros((1, ML_HEADS, ML_DV, ML_DK), F32), jnp.zeros((1, ML_HEADS, ML_DK), F32),
                jnp.zeros((1, 1, LANE), F32)]

    wts = {
        "w_in": w_in,
        "w_in_tail": _tail_cols(w_in).astype(BF16),
        "w_out": w_out,
        "w_gate": ffn_w_gate,
        "w_up": ffn_w_up,
        "w_down": ffn_w_down.astype(BF16),
    }

    p_out, s_out = [], []
    for l in range(depth):
        lp = {
            "norm_mix_w": norm_mix_w[l],
            "mixers": [row(rwkv_mu[l]), row(rwkv_w0[l]), rwkv_w_up[l], row(rwkv_a0[l]), rwkv_a_up[l], rwkv_g_up[l],
                       row(rwkv_k_k[l]), row(rwkv_k_a[l]), row(rwkv_r_k[l]), row(rwkv_ln_w[l]), row(rwkv_ln_b[l]),
                       gdn_conv_w[l], _lane_vec(gdn_a_log[l]), _lane_vec(gdn_dt_bias[l]), row(gdn_norm_w[l]),
                       _lane_vec(mlstm_i_b[l]), _lane_vec(mlstm_f_b[l], ML_HEADS), row(mlstm_norm_w[l])],
            "norm_ffn_w": norm_ffn_w[l],
            "ffn": [ffn_conv_w[l], row(ffn_conv_b[l])],
        }
        sample_mix = [state_rwkv_shift[l][:, None, :], state_rwkv_wkv[l],
                      _pad_tail(cache_gdn_conv[l], 3 * GD_W), state_gdn[l],
                      state_mlstm_c[l], state_mlstm_n[l], _pad_cols(state_mlstm_m[l], LANE)[:, None, :]]
        states = ([zero_mix, None], [None, None], [sample_mix, _pad_tail(cache_ffn_conv[l], D_FF)])
        x, st = _layer(x, states, lp, wts, l, segs)
        p_out.append(st[1])
        s_out.append(st[2])

    y_prompt = _rmsnorm(x, final_norm_w, F32, seq, ROW_PROMPT, CHUNK).reshape(1, seq, D_MODEL)
    y_sample = _rmsnorm(x, final_norm_w, F32, dec_b * dec_seq, row_sample, CHUNK).reshape(dec_b, dec_seq, D_MODEL)

    def collect(sts):
        stack = lambda f: jnp.stack([f(st) for st in sts], 0)
        return (stack(lambda st: st[0][1]), stack(lambda st: st[0][0][:, 0, :]),
                stack(lambda st: st[0][3]), stack(lambda st: st[0][2][:, TAIL - 3:, :]),
                stack(lambda st: st[0][4]), stack(lambda st: st[0][5]), stack(lambda st: st[0][6][:, 0, :ML_HEADS]),
                stack(lambda st: st[1][:, TAIL - 2:, :]))

    return (y_prompt, y_sample) + collect(p_out) + collect(s_out)
```

```python
import functools
import math

import jax
import jax.numpy as jnp
from jax import lax
from jax.experimental import pallas as pl
from jax.experimental.pallas import tpu as pltpu

F32 = jnp.float32
BF16 = jnp.bfloat16

D_MODEL = 4096
N_META = 16
CHUNK = 64
NORM_EPS = 1e-6
RW_HEADS, RW_HD = 24, 64
RW_W = RW_HEADS * RW_HD
RW_P = 3 * RW_W + 64 + 64 + 128
RW_GN_EPS = 64e-5
GD_HEADS, GD_HD = 12, 128
GD_W = GD_HEADS * GD_HD
GD_P = 4 * GD_W + 2 * GD_HEADS
ML_HEADS, ML_DK, ML_DV = 4, 128, 256
ML_W = ML_HEADS * ML_DV
ML_QK = ML_HEADS * ML_DK
ML_P = 2 * ML_QK + 2 * ML_W + 2 * ML_HEADS
D_FF = 11008

LANE = 128
TAIL = 8
ROW_META = 48
ROW_PROMPT = 64
ROW_TILE = 2128
ROW_TILE_DOWN = 1216
ROW_TILE_SMALL = 448
COL_TILE = 256
FFN_TAIL_ROWS = 512
VMEM_LIMIT = 56 * 1024 * 1024

P_MAIN = RW_P + 3 * GD_W
T_AB = 0
T_IF = 128
T_MLQ = 256
T_MLK = 768
T_Z = 1280
T_MLV = 2816
T_MLO = 3840
P_TAIL = 4864


def _cparams(n_axes):
    return pltpu.CompilerParams(dimension_semantics=("arbitrary",) * n_axes, vmem_limit_bytes=VMEM_LIMIT)


def _bdot(a, b):
    return jnp.dot(a.astype(BF16), b.astype(BF16), preferred_element_type=F32)


def _bdot_nt(a, b):
    return lax.dot_general(a.astype(BF16), b.astype(BF16), (((1,), (1,)), ((), ())), preferred_element_type=F32)


def _bdot_tn(a, b):
    return lax.dot_general(a.astype(BF16), b.astype(BF16), (((0,), (0,)), ((), ())), preferred_element_type=F32)


def _split3(x):
    hi = x.astype(BF16)
    r1 = x - hi.astype(F32)
    mid = r1.astype(BF16)
    lo = (r1 - mid.astype(F32)).astype(BF16)
    return hi, mid, lo


def _cumsum_rows(x, tril_bf):
    hi, mid, lo = _split3(x)
    d = lambda t: jnp.dot(tril_bf, t, preferred_element_type=F32)
    return d(hi) + d(mid) + d(lo)


def _cols_to_rows(x, n_rows):
    sel = (lax.broadcasted_iota(jnp.int32, (n_rows, LANE), 0) == lax.broadcasted_iota(jnp.int32, (n_rows, LANE), 1)).astype(BF16)
    hi, mid, lo = _split3(x)
    d = lambda t: lax.dot_general(sel, t, (((1,), (1,)), ((), ())), preferred_element_type=F32)
    return d(hi) + d(mid) + d(lo)


def _head_selectors(width, hd):
    sh = int(math.log2(hd))
    e = (lax.broadcasted_iota(jnp.int32, (width, LANE), 0) >> sh) == lax.broadcasted_iota(jnp.int32, (width, LANE), 1)
    et = lax.broadcasted_iota(jnp.int32, (LANE, width), 0) == (lax.broadcasted_iota(jnp.int32, (LANE, width), 1) >> sh)
    return e.astype(BF16), et.astype(BF16)


def _head_sums(x, sel):
    e, et = sel
    hi, lo, _ = _split3(x)
    d = lambda t: jnp.dot(t, e, preferred_element_type=F32)
    hi, lo, _ = _split3(d(hi) + d(lo))
    d = lambda t: jnp.dot(t, et, preferred_element_type=F32)
    return d(hi) + d(lo)


def _tri_masks(C):
    ri = lax.broadcasted_iota(jnp.int32, (C, C), 0)
    ci = lax.broadcasted_iota(jnp.int32, (C, C), 1)
    return ri >= ci, ri > ci, ri == ci


def _unit_lower_inverse(As, eye, C):
    Ps = [-A for A in As]
    Ts = [jnp.where(eye, 1.0, P) for P in Ps]
    for _ in range(int(math.log2(C)) - 1):
        Ps = [_bdot(P, P) for P in Ps]
        Ts = [T + _bdot(T, P) for T, P in zip(Ts, Ps)]
        yield
    return Ts


def _interleave(gens):
    results, alive = [None] * len(gens), list(range(len(gens)))
    while alive:
        for idx in list(alive):
            try:
                next(gens[idx])
            except StopIteration as stop:
                results[idx] = stop.value
                alive.remove(idx)
    return results


def _sigmoid(x):
    return jax.nn.sigmoid(x)


def _softplus(x):
    return jnp.maximum(x, 0.0) + jnp.log(1.0 + jnp.exp(-jnp.abs(x)))


def _shift_rows(x, tail, s, C):
    xr = pltpu.roll(x, s, 0)
    r8 = lax.broadcasted_iota(jnp.int32, (TAIL, 1), 0)
    top = jnp.where(r8 < s, pltpu.roll(tail, s, 0), xr[0:TAIL])
    return jnp.concatenate([top, xr[TAIL:]], axis=0)


def _rmsnorm_body(x_ref, w_ref, o_ref):
    x = x_ref[...]
    y = x * lax.rsqrt(jnp.mean(x * x, -1, keepdims=True) + NORM_EPS)
    o_ref[...] = (y * w_ref[...]).astype(o_ref.dtype)


def _rmsnorm(x, w, out_dtype, rows, row_base, block):
    d = x.shape[1]
    rb = row_base // block
    return pl.pallas_call(
        _rmsnorm_body,
        grid=(rows // block,),
        in_specs=[pl.BlockSpec((block, d), lambda i: (rb + i, 0)), pl.BlockSpec((1, d), lambda i: (0, 0))],
        out_specs=pl.BlockSpec((block, d), lambda i: (i, 0)),
        out_shape=jax.ShapeDtypeStruct((rows, d), out_dtype),
        compiler_params=_cparams(1),
        name="rmsnorm",
    )(x, w.reshape(1, d))


def _mm_body(a_ref, w_ref, o_ref):
    o_ref[...] = jnp.dot(a_ref[...], w_ref[...].astype(BF16), preferred_element_type=F32)


def _mm_res_body(a_ref, w_ref, r_ref, o_ref):
    o_ref[...] = r_ref[...] + jnp.dot(a_ref[...], w_ref[...].astype(BF16), preferred_element_type=F32)


def _matmul_wide(a, w, layer, name="matmul_wide"):
    m, k = a.shape
    n = w.shape[2]
    tm, tn = ROW_TILE_SMALL, 4 * COL_TILE
    return pl.pallas_call(
        _mm_body, grid=(n // tn, m // tm),
        in_specs=[pl.BlockSpec((tm, k), lambda j, i: (i, 0)), pl.BlockSpec((None, k, tn), lambda j, i: (layer, 0, j))],
        out_specs=pl.BlockSpec((tm, tn), lambda j, i: (i, j)),
        out_shape=jax.ShapeDtypeStruct((m, n), F32),
        compiler_params=_cparams(2), name=name,
    )(a, w)


def _matmul(a, w, layer, tm, res=None, name="matmul"):
    m, k = a.shape
    n = w.shape[2]
    a_spec = pl.BlockSpec((tm, k), lambda i, j: (i, 0), pipeline_mode=pl.Buffered(1))
    w_spec = pl.BlockSpec((None, k, COL_TILE), lambda i, j: (layer, 0, j))
    o_spec = pl.BlockSpec((tm, COL_TILE), lambda i, j: (i, j))
    if res is None:
        body, specs, args = _mm_body, [a_spec, w_spec], (a, w)
    else:
        body, specs, args = _mm_res_body, [a_spec, w_spec, o_spec], (a, w, res)
    return pl.pallas_call(
        body, grid=(m // tm, n // COL_TILE), in_specs=specs, out_specs=o_spec,
        out_shape=jax.ShapeDtypeStruct((m, n), F32),
        compiler_params=_cparams(2), name=name,
    )(*args)


def _row_spec(width, col_start, seg):
    row_base, C, B, nblk = seg
    rb, cb = row_base // C, col_start // width
    return pl.BlockSpec((C, width), lambda b, i: (rb + b * nblk + i, cb))


def _state_spec(shape):
    nd = len(shape)
    return pl.BlockSpec((1,) + tuple(shape[1:]), lambda b, i: (b,) + (0,) * (nd - 1))


def _param_spec(shape):
    nd = len(shape)
    return pl.BlockSpec(tuple(shape), lambda b, i: (0,) * nd)


def _seq_call(body, seg, row_ins, state_ins, params, buf, out_width, out_col, scratch, name, in_row_base=None):
    row_base, C, B, nblk = seg
    in_seg = seg if in_row_base is None else (in_row_base, C, B, nblk)
    in_specs = [_row_spec(w, c, in_seg) for (_, w, c) in row_ins]
    in_specs += [_state_spec(s.shape) for s in state_ins]
    in_specs += [_param_spec(p.shape) for p in params]
    in_specs += [pl.BlockSpec(memory_space=pl.ANY)]
    args = [a for (a, _, _) in row_ins] + list(state_ins) + list(params) + [buf]
    out_specs = [_row_spec(out_width, out_col, seg)] + [_state_spec(s.shape) for s in state_ins]
    out_shape = [jax.ShapeDtypeStruct(buf.shape, buf.dtype)] + [jax.ShapeDtypeStruct(s.shape, s.dtype) for s in state_ins]
    return pl.pallas_call(
        functools.partial(body, C, nblk), grid=(B, nblk), in_specs=in_specs, out_specs=out_specs,
        out_shape=out_shape, scratch_shapes=scratch,
        input_output_aliases={len(args) - 1: 0},
        compiler_params=_cparams(2), name=name,
    )(*args)


def _rwkv_main(C, p, carry, s_scr, prm):
    mu, w0, wup, a0, aup, gup, kkw, kaw, rkw, lnw, lnb = prm
    rows = lax.broadcasted_iota(jnp.int32, (C, 1), 0)
    prev = jnp.where(rows == 0, carry[...], pltpu.roll(p, 1, 0))
    carry[...] = p[C - 1:C, :]
    xs = p + (prev - p) * mu[...]
    r, k, v = xs[:, 0:RW_W], xs[:, RW_W:2 * RW_W], xs[:, 2 * RW_W:3 * RW_W]
    dw, da, dg = xs[:, 3 * RW_W:3 * RW_W + 64], xs[:, 3 * RW_W + 64:3 * RW_W + 128], xs[:, 3 * RW_W + 128:RW_P]
    lw = -math.exp(-0.5) * _sigmoid(w0[...] + _bdot(jnp.tanh(dw), wup[...]))
    a = _sigmoid(a0[...] + _bdot(da, aup[...]))
    g = _bdot(_sigmoid(dg), gup[...])
    sel = _head_selectors(RW_W, RW_HD)
    kkx = k * kkw[...]
    kkn = kkx * lax.rsqrt(_head_sums(kkx * kkx, sel) + 1e-6)
    kt = k * (1.0 + (a - 1.0) * kaw[...])
    bonus = _head_sums(r * kt * rkw[...], sel) * v
    kb = kkn * a
    yield

    tril, strict, eye = _tri_masks(C)
    cl = _cumsum_rows(lw, tril.astype(BF16))
    w_in = jnp.exp(cl)
    w_ex = jnp.exp(cl - lw)
    w_inv = jnp.exp(-cl)
    w_end = jnp.exp(cl[C - 1:C, :] - cl)

    heads = range(RW_HEADS)
    sls = [slice(RW_HD * h, RW_HD * (h + 1)) for h in heads]
    kk_ex, r_in = kkn * w_ex, r * w_in
    kt_inv, kb_inv = kt * w_inv, kb * w_inv
    kt_end, kb_end = kt * w_end, kb * w_end
    yield
    lhss = [jnp.concatenate([kk_ex[:, sl], r_in[:, sl]], axis=0) for sl in sls]
    rhss = [jnp.concatenate([kt_inv[:, sl], kb_inv[:, sl]], axis=0) for sl in sls]
    scs = [_bdot_nt(lhs, rhs) for lhs, rhs in zip(lhss, rhss)]
    yield
    s0s = [s_scr[h] for h in heads]
    pss = [_bdot_nt(lhs, s0) for lhs, s0 in zip(lhss, s0s)]
    yield
    negs = [-(ps[:C] + _bdot(jnp.where(strict, sc[:C, :C], 0.0), v[:, sl])) for ps, sc, sl in zip(pss, scs, sls)]
    yield
    invs = yield from _unit_lower_inverse([jnp.where(strict, sc[:C, C:], 0.0) for sc in scs], eye, C)
    us = [_bdot(t, n) for t, n in zip(invs, negs)]
    yield
    vus = [jnp.concatenate([v[:, sl], u], axis=0) for u, sl in zip(us, sls)]
    a_rs = [jnp.concatenate([jnp.where(tril, sc[C:, :C], 0.0), jnp.where(tril, sc[C:, C:], 0.0)], axis=1) for sc in scs]
    ys = [ps[C:] + _bdot(a_r, vu) for ps, a_r, vu in zip(pss, a_rs, vus)]
    yield
    kes = [jnp.concatenate([kt_end[:, sl], kb_end[:, sl]], axis=0) for sl in sls]
    for h in heads:
        s_scr[h] = s0s[h] * w_in[C - 1:C, sls[h]] + _bdot_tn(vus[h], kes[h])
    yield
    y = jnp.concatenate(ys, axis=1)
    yc = y - _head_sums(y, sel) * (1.0 / RW_HD)
    yn = yc * lax.rsqrt(_head_sums(yc * yc, sel) * (1.0 / RW_HD) + RW_GN_EPS)
    return (yn * lnw[...] + lnb[...] + bonus) * g


def _gdn_main(C, x, z, ab, tail, s_scr, prm):
    convw, alog, dtb, normw = prm
    t8 = tail[...]
    conv = x * convw[3:4, :]
    for s in (1, 2, 3):
        conv = conv + _shift_rows(x, t8, s, C) * convw[3 - s:4 - s, :]
    tail[...] = x[C - TAIL:, :]
    act = conv * _sigmoid(conv)
    q, k, v = act[:, 0:GD_W], act[:, GD_W:2 * GD_W], act[:, 2 * GD_W:3 * GD_W]
    yield

    g_all = -jnp.exp(alog[...]) * _softplus(ab + dtb[...])
    beta_all = _sigmoid(ab)
    tril, strict, eye = _tri_masks(C)
    gc = _cumsum_rows(g_all, tril.astype(BF16))
    gr = _cols_to_rows(gc, 16)
    eg = jnp.exp(gc)
    yield

    heads = range(GD_HEADS)
    sls = [slice(GD_HD * h, GD_HD * (h + 1)) for h in heads]
    qs = [q[:, sl] for sl in sls]
    qs = [t * (lax.rsqrt(jnp.sum(t * t, -1, keepdims=True) + 1e-6) * GD_HD ** -0.5) for t in qs]
    ks = [k[:, sl] for sl in sls]
    ks = [t * lax.rsqrt(jnp.sum(t * t, -1, keepdims=True) + 1e-6) for t in ks]
    yield
    betas = [beta_all[:, GD_HEADS + h:GD_HEADS + h + 1] for h in heads]
    decays = [jnp.where(tril, jnp.exp(jnp.where(tril, gc[:, h:h + 1] - gr[h:h + 1, :], 0.0)), 0.0) for h in heads]
    yield
    scs = [_bdot_nt(jnp.concatenate([k_h, q_h], axis=0), k_h) for k_h, q_h in zip(ks, qs)]
    yield
    invs = yield from _unit_lower_inverse(
        [jnp.where(strict, beta * sc[:C] * decay, 0.0) for beta, sc, decay in zip(betas, scs, decays)], eye, C)
    rhss = [jnp.concatenate([v[:, sl] * beta, k_h * (beta * eg[:, h:h + 1])], axis=1)
            for h, sl, beta, k_h in zip(heads, sls, betas, ks)]
    sols = [_bdot(t, rhs) for t, rhs in zip(invs, rhss)]
    yield
    s0s = [s_scr[h] for h in heads]
    pss = [_bdot_nt(jnp.concatenate([sol[:, GD_HD:], q_h], axis=0), s0) for sol, q_h, s0 in zip(sols, qs, s0s)]
    yield
    us = [sol[:, :GD_HD] - ps[:C] for sol, ps in zip(sols, pss)]
    os_ = [eg[:, h:h + 1] * ps[C:] + _bdot(sc[C:] * decay, u) for h, ps, sc, decay, u in zip(heads, pss, scs, decays, us)]
    yield
    for h in heads:
        gl = gc[C - 1:C, h:h + 1]
        s_scr[h] = jnp.exp(gl) * s0s[h] + _bdot_tn(us[h] * jnp.exp(gl - gc[:, h:h + 1]), ks[h])
    yield
    outs = [o * lax.rsqrt(jnp.mean(o * o, -1, keepdims=True) + NORM_EPS) * normw[...] for o in os_]
    return jnp.concatenate(outs, axis=1) * (z * _sigmoid(z))


def _mlstm_main(C, q, k, v, og, gates, c_scr, n_scr, m_scr, prm):
    ib, fb, normw = prm
    li_all = gates + ib[...]
    x = gates + fb[...]
    lf_all = jnp.minimum(x, 0.0) - jnp.log(1.0 + jnp.exp(-jnp.abs(x)))
    tril, _, _ = _tri_masks(C)
    fc = _cumsum_rows(lf_all, tril.astype(BF16))
    fr = _cols_to_rows(fc, 8)
    lir = _cols_to_rows(li_all, 8)
    lane = lax.broadcasted_iota(jnp.int32, (1, LANE), 1)
    m_all = m_scr[...]
    m_new = m_all
    yield

    heads = range(ML_HEADS)
    qs = [q[:, ML_DK * h:ML_DK * (h + 1)] * ML_DK ** -0.5 for h in heads]
    ks = [k[:, ML_DK * h:ML_DK * (h + 1)] for h in heads]
    vs = [v[:, ML_DV * h:ML_DV * (h + 1)] for h in heads]
    c0s = [c_scr[h] for h in heads]
    n0s = [n_scr[h:h + 1, :] for h in heads]
    qks = [_bdot_nt(q_h, k_h) for q_h, k_h in zip(qs, ks)]
    qcs = [_bdot_nt(q_h, c0) for q_h, c0 in zip(qs, c0s)]
    yield
    fcs = [fc[:, ML_HEADS + h:ML_HEADS + h + 1] for h in heads]
    m0s = [m_all[:, h:h + 1] for h in heads]
    dmats = [jnp.where(tril, fc_h - fr[ML_HEADS + h:ML_HEADS + h + 1, :] + lir[h:h + 1, :], -jnp.inf)
             for h, fc_h in zip(heads, fcs)]
    inters = [fc_h + m0 for fc_h, m0 in zip(fcs, m0s)]
    ms = [jnp.maximum(inter, jnp.max(dmat, -1, keepdims=True)) for inter, dmat in zip(inters, dmats)]
    yield
    ss = [qk * jnp.exp(dmat - m) for qk, dmat, m in zip(qks, dmats, ms)]
    wis = [jnp.exp(inter - m) for inter, m in zip(inters, ms)]
    yield
    nums = [wi * qc + _bdot(s, v_h) for wi, qc, s, v_h in zip(wis, qcs, ss, vs)]
    dens = [wi * jnp.sum(q_h * n0, -1, keepdims=True) + jnp.sum(s, -1, keepdims=True)
            for wi, q_h, n0, s in zip(wis, qs, n0s, ss)]
    yield
    hhs = [num / jnp.maximum(jnp.abs(den), jnp.exp(-m)) for num, den, m in zip(nums, dens, ms)]
    m_cs = [m[C - 1:C, :] for m in ms]
    wends = [jnp.exp(fc_h[C - 1:C, :] - fc_h + li_all[:, h:h + 1] - m_c) for h, fc_h, m_c in zip(heads, fcs, m_cs)]
    dstates = [jnp.exp(fc_h[C - 1:C, :] + m0 - m_c) for fc_h, m0, m_c in zip(fcs, m0s, m_cs)]
    for h in heads:
        c_scr[h] = dstates[h] * c0s[h] + _bdot_tn(vs[h] * wends[h], ks[h])
        n_scr[h:h + 1, :] = dstates[h] * n0s[h] + jnp.sum(wends[h] * ks[h], 0, keepdims=True)
        m_new = jnp.where(lane == h, m_cs[h], m_new)
    m_scr[...] = m_new
    yield
    outs = [hh * lax.rsqrt(jnp.mean(hh * hh, -1, keepdims=True) + NORM_EPS) for hh in hhs]
    return jnp.concatenate(outs, axis=1) * normw[...] * _sigmoid(og)


N_MIX_STATES = 7
N_RW_PRM, N_GD_PRM, N_ML_PRM = 11, 4, 3


def _mixer_body(C, nblk, p_ref, *refs):
    ins, refs = refs[:N_MIX_STATES], refs[N_MIX_STATES:]
    rw_prm, refs = refs[:N_RW_PRM], refs[N_RW_PRM:]
    gd_prm, refs = refs[:N_GD_PRM], refs[N_GD_PRM:]
    ml_prm, refs = refs[:N_ML_PRM], refs[N_ML_PRM:]
    o_ref, refs = refs[1], refs[2:]
    outs, scr = refs[:N_MIX_STATES], refs[N_MIX_STATES:]
    carry, s_rw, tail, s_gd, c_scr, n_scr, m_scr = scr
    i = pl.program_id(1)

    @pl.when(i == 0)
    def _():
        for dst, src in zip(scr, ins):
            dst[...] = src[0]

    pm, pt = p_ref[:, 0:P_MAIN], p_ref[:, P_MAIN:P_MAIN + P_TAIL]
    o_rw, o_gd, o_ml = _interleave([
        _rwkv_main(C, pm[:, 0:RW_P], carry, s_rw, rw_prm),
        _gdn_main(C, pm[:, RW_P:P_MAIN], pt[:, T_Z:T_Z + GD_W], pt[:, T_AB:T_AB + LANE], tail, s_gd, gd_prm),
        _mlstm_main(C, pt[:, T_MLQ:T_MLQ + ML_QK], pt[:, T_MLK:T_MLK + ML_QK], pt[:, T_MLV:T_MLV + ML_W],
                    pt[:, T_MLO:T_MLO + ML_W], pt[:, T_IF:T_IF + LANE], c_scr, n_scr, m_scr, ml_prm)])
    o_ref[:, 0:RW_W] = o_rw.astype(o_ref.dtype)
    o_ref[:, RW_W:RW_W + GD_W] = o_gd.astype(o_ref.dtype)
    o_ref[:, RW_W + GD_W:] = o_ml.astype(o_ref.dtype)

    @pl.when(i == nblk - 1)
    def _():
        outs[0][0] = pm[C - 1:C, 0:RW_P]
        for dst, src in zip(outs[1:], scr[1:]):
            dst[0] = src[...]


def _mixer_call(p, buf, seg, states, params):
    scratch = [pltpu.VMEM((1, RW_P), F32), pltpu.VMEM((RW_HEADS, RW_HD, RW_HD), F32),
               pltpu.VMEM((TAIL, 3 * GD_W), F32), pltpu.VMEM((GD_HEADS, GD_HD, GD_HD), F32),
               pltpu.VMEM((ML_HEADS, ML_DV, ML_DK), F32), pltpu.VMEM((ML_HEADS, ML_DK), F32), pltpu.VMEM((1, LANE), F32)]
    outs = _seq_call(_mixer_body, seg, [(p, P_MAIN + P_TAIL, 0)], list(states), params,
                     buf, D_MODEL, 0, scratch, "mixers")
    return outs[0], list(outs[1:])


def _ffn_act_body(C, nblk, g_ref, u_ref, tail_in, convw, convb, _buf, o_ref, tail_out, tail):
    i = pl.program_id(1)

    @pl.when(i == 0)
    def _():
        tail[...] = tail_in[0]

    x = g_ref[...]
    t8 = tail[...]
    conv = x * convw[2:3, :] + convb[...]
    for s in (1, 2):
        conv = conv + _shift_rows(x, t8, s, C) * convw[2 - s:3 - s, :]
    tail[...] = x[C - TAIL:, :]
    o_ref[...] = (conv * _sigmoid(conv) * u_ref[...]).astype(o_ref.dtype)

    @pl.when(i == nblk - 1)
    def _():
        tail_out[0] = tail[...]


def _ffn_act_call(g, u, buf, seg, tail, params, in_row_base=None):
    outs = _seq_call(
        _ffn_act_body, seg, [(g, D_FF, 0), (u, D_FF, 0)], [tail], params, buf, D_FF, 0,
        [pltpu.VMEM((TAIL, D_FF), F32)], "ffn_act", in_row_base)
    return outs[0], outs[1]


def _ffn_up_body(tm, h_ref, wg_ref, wu_ref, convw, convb, o_ref, gt_ref, ut_ref, carry):
    i, j = pl.program_id(0), pl.program_id(1)
    h = h_ref[...]
    g = jnp.dot(h, wg_ref[...].astype(BF16), preferred_element_type=F32)
    u = jnp.dot(h, wu_ref[...].astype(BF16), preferred_element_type=F32)
    gt_ref[...] = g[tm - FFN_TAIL_ROWS:, :]
    ut_ref[...] = u[tm - FFN_TAIL_ROWS:, :]
    rows = i * tm + lax.broadcasted_iota(jnp.int32, (tm, 1), 0)
    g = jnp.where(rows >= ROW_META, g, 0.0)

    @pl.when(i == 0)
    def _():
        carry[j] = jnp.zeros((TAIL, COL_TILE), F32)

    t8 = carry[j]
    conv = g * convw[2:3, :] + convb[...]
    for s in (1, 2):
        conv = conv + _shift_rows(g, t8, s, tm) * convw[2 - s:3 - s, :]
    carry[j] = g[tm - TAIL:, :]
    o_ref[...] = (conv * _sigmoid(conv) * u).astype(o_ref.dtype)


def _ffn_up(hb, w_gate, w_up, layer, convw, convb, tm):
    m, k = hb.shape
    n = w_gate.shape[2]
    w_spec = pl.BlockSpec((None, k, COL_TILE), lambda i, j: (layer, 0, j))
    t_spec = pl.BlockSpec((FFN_TAIL_ROWS, COL_TILE), lambda i, j: (i, j))
    return pl.pallas_call(
        functools.partial(_ffn_up_body, tm), grid=(m // tm, n // COL_TILE),
        in_specs=[pl.BlockSpec((tm, k), lambda i, j: (i, 0), pipeline_mode=pl.Buffered(1)), w_spec, w_spec,
                  pl.BlockSpec((3, COL_TILE), lambda i, j: (0, j)), pl.BlockSpec((1, COL_TILE), lambda i, j: (0, j))],
        out_specs=[pl.BlockSpec((tm, COL_TILE), lambda i, j: (i, j)), t_spec, t_spec],
        out_shape=[jax.ShapeDtypeStruct((m, n), BF16), jax.ShapeDtypeStruct((m // tm * FFN_TAIL_ROWS, n), F32),
                   jax.ShapeDtypeStruct((m // tm * FFN_TAIL_ROWS, n), F32)],
        scratch_shapes=[pltpu.VMEM((n // COL_TILE, TAIL, COL_TILE), F32)],
        compiler_params=_cparams(2), name="ffn_up_act",
    )(hb, w_gate, w_up, convw, convb)


def _pad_cols(x, width):
    return jnp.pad(x, ((0, 0),) * (x.ndim - 1) + ((0, width - x.shape[-1]),))


def _lane_vec(x, start=0):
    return jnp.zeros((1, LANE), F32).at[0, start:start + x.shape[0]].set(x)


def _tail_cols(w):
    ml0 = RW_P + GD_P
    gd_ab = w[..., P_MAIN:P_MAIN + 2 * GD_HEADS]
    ml_if = w[..., ml0 + 2 * ML_QK + ML_W:ml0 + 2 * ML_QK + ML_W + 2 * ML_HEADS]
    cols = [_pad_cols(gd_ab, LANE), _pad_cols(ml_if, LANE), w[..., ml0:ml0 + 2 * ML_QK],
            w[..., P_MAIN + 2 * GD_HEADS:ml0], w[..., ml0 + 2 * ML_QK:ml0 + 2 * ML_QK + ML_W],
            w[..., ml0 + 2 * ML_QK + ML_W + 2 * ML_HEADS:ml0 + ML_P]]
    return jnp.concatenate(cols, axis=-1)


def _pad_tail(buf, width):
    b, r, w = buf.shape
    return jnp.pad(buf, ((0, 0), (TAIL - r, 0), (0, width - w)))


def _layer(x, states, lp, wts, layer, segs):
    n = x.shape[0]
    hb = _rmsnorm(x, lp["norm_mix_w"], BF16, n, 0, ROW_TILE_SMALL)
    p = _matmul_wide(hb, wts["w_in"], layer, name="proj_in")
    mix = jnp.zeros((n, D_MODEL), BF16)
    new_states = []
    for seg, (mix_st, ffn_tail) in zip(segs, states):
        if mix_st is None:
            mix_st = new_states[-1][0]
        mix, mix_new = _mixer_call(p, mix, seg, mix_st, lp["mixers"])
        new_states.append([mix_new, ffn_tail])
    x = _matmul(mix, wts["w_out"], layer, ROW_TILE, res=x, name="proj_out")
    hb = _rmsnorm(x, lp["norm_ffn_w"], BF16, n, 0, ROW_TILE_SMALL)
    act, g_tail, u_tail = _ffn_up(hb, wts["w_gate"], wts["w_up"], layer, lp["ffn"][0], lp["ffn"][1], ROW_TILE)
    tail_base = n - g_tail.shape[0]
    for idx, seg in enumerate(segs):
        if idx == 0 or states[idx][0] is None:
            end = seg[0] + seg[1] * seg[2] * seg[3]
            kept = end - TAIL >= n - FFN_TAIL_ROWS
            new_states[idx][1] = g_tail[end - TAIL - tail_base:end - tail_base][None] if kept else None
        else:
            assert seg[0] >= n - FFN_TAIL_ROWS
            act, new_states[idx][1] = _ffn_act_call(g_tail, u_tail, act, seg, new_states[idx][1], lp["ffn"],
                                                    in_row_base=seg[0] - tail_base)
    x = _matmul(act, wts["w_down"], layer, ROW_TILE_DOWN, res=x, name="ffn_down")
    return x, new_states


def kernel(x_prompt, x_sample, state_rwkv_wkv, state_rwkv_shift, state_gdn, cache_gdn_conv, state_mlstm_c, state_mlstm_n, state_mlstm_m, cache_ffn_conv, meta_tokens, norm_mix_w, w_in, rwkv_mu, rwkv_w0, rwkv_w_up, rwkv_a0, rwkv_a_up, rwkv_g_up, rwkv_k_k, rwkv_k_a, rwkv_r_k, rwkv_ln_w, rwkv_ln_b, gdn_conv_w, gdn_a_log, gdn_dt_bias, gdn_norm_w, mlstm_i_b, mlstm_f_b, mlstm_norm_w, w_out, norm_ffn_w, ffn_w_gate, ffn_w_up, ffn_conv_w, ffn_conv_b, ffn_w_down, final_norm_w):
    depth = w_in.shape[0]
    seq = x_prompt.shape[1]
    dec_b, dec_seq = x_sample.shape[0], x_sample.shape[1]
    row_sample = ROW_PROMPT + seq
    n_rows = row_sample + dec_b * dec_seq
    n_pad = -(-n_rows // ROW_TILE) * ROW_TILE
    segs = ((ROW_META, N_META, 1, 1), (ROW_PROMPT, CHUNK, 1, seq // CHUNK), (row_sample, dec_seq, dec_b, 1))

    x = jnp.concatenate([
        jnp.zeros((ROW_META, D_MODEL), F32), meta_tokens.astype(F32), x_prompt[0],
        x_sample.reshape(dec_b * dec_seq, D_MODEL), jnp.zeros((n_pad - n_rows, D_MODEL), F32)], axis=0)

    row = lambda a: a.reshape(1, -1)
    zero_mix = [jnp.zeros((1, 1, RW_P), F32), jnp.zeros((1, RW_HEADS, RW_HD, RW_HD), F32),
                jnp.zeros((1, TAIL, 3 * GD_W), F32), jnp.zeros((1, GD_HEADS, GD_HD, GD_HD), F32),
                jnp.zeros((1, ML_HEADS, ML_DV, ML_DK), F32), jnp.zeros((1, ML_HEADS, ML_DK), F32),
                jnp.zeros((1, 1, LANE), F32)]

    wts = {
        "w_in": jnp.concatenate([w_in[..., :P_MAIN], _tail_cols(w_in)], axis=-1).astype(BF16),
        "w_out": w_out,
        "w_gate": ffn_w_gate,
        "w_up": ffn_w_up,
        "w_down": ffn_w_down.astype(BF16),
    }

    p_out, s_out = [], []
    for l in range(depth):
        lp = {
            "norm_mix_w": norm_mix_w[l],
            "mixers": [row(rwkv_mu[l]), row(rwkv_w0[l]), rwkv_w_up[l], row(rwkv_a0[l]), rwkv_a_up[l], rwkv_g_up[l],
                       row(rwkv_k_k[l]), row(rwkv_k_a[l]), row(rwkv_r_k[l]), row(rwkv_ln_w[l]), row(rwkv_ln_b[l]),
                       gdn_conv_w[l], _lane_vec(gdn_a_log[l]), _lane_vec(gdn_dt_bias[l]), row(gdn_norm_w[l]),
                       _lane_vec(mlstm_i_b[l]), _lane_vec(mlstm_f_b[l], ML_HEADS), row(mlstm_norm_w[l])],
            "norm_ffn_w": norm_ffn_w[l],
            "ffn": [ffn_conv_w[l], row(ffn_conv_b[l])],
        }
        sample_mix = [state_rwkv_shift[l][:, None, :], state_rwkv_wkv[l],
                      _pad_tail(cache_gdn_conv[l], 3 * GD_W), state_gdn[l],
                      state_mlstm_c[l], state_mlstm_n[l], _pad_cols(state_mlstm_m[l], LANE)[:, None, :]]
        states = ([zero_mix, None], [None, None], [sample_mix, _pad_tail(cache_ffn_conv[l], D_FF)])
        x, st = _layer(x, states, lp, wts, l, segs)
        p_out.append(st[1])
        s_out.append(st[2])

    y_prompt = _rmsnorm(x, final_norm_w, F32, seq, ROW_PROMPT, CHUNK).reshape(1, seq, D_MODEL)
    y_sample = _rmsnorm(x, final_norm_w, F32, dec_b * dec_seq, row_sample, CHUNK).reshape(dec_b, dec_seq, D_MODEL)

    def collect(sts):
        stack = lambda f: jnp.stack([f(st) for st in sts], 0)
        return (stack(lambda st: st[0][1]), stack(lambda st: st[0][0][:, 0, :]),
                stack(lambda st: st[0][3]), stack(lambda st: st[0][2][:, TAIL - 3:, :]),
                stack(lambda st: st[0][4]), stack(lambda st: st[0][5]), stack(lambda st: st[0][6][:, 0, :ML_HEADS]),
                stack(lambda st: st[1][:, TAIL - 2:, :]))

    return (y_prompt, y_sample) + collect(p_out) + collect(s_out)
```

```python
import functools
import math

import jax
import jax.numpy as jnp
from jax import lax
from jax.experimental import pallas as pl
from jax.experimental.pallas import tpu as pltpu

F32 = jnp.float32
BF16 = jnp.bfloat16

D_MODEL = 4096
N_META = 16
CHUNK = 64
NORM_EPS = 1e-6
RW_HEADS, RW_HD = 24, 64
RW_W = RW_HEADS * RW_HD
RW_P = 3 * RW_W + 64 + 64 + 128
RW_GN_EPS = 64e-5
GD_HEADS, GD_HD = 12, 128
GD_W = GD_HEADS * GD_HD
GD_P = 4 * GD_W + 2 * GD_HEADS
ML_HEADS, ML_DK, ML_DV = 4, 128, 256
ML_W = ML_HEADS * ML_DV
ML_QK = ML_HEADS * ML_DK
ML_P = 2 * ML_QK + 2 * ML_W + 2 * ML_HEADS
D_FF = 11008

LANE = 128
TAIL = 8
ROW_META = 48
ROW_PROMPT = 64
ROW_TILE = 2128
ROW_TILE_DOWN = 1216
ROW_TILE_SMALL = 448
COL_TILE = 256
FFN_TAIL_ROWS = 512
VMEM_LIMIT = 56 * 1024 * 1024

P_MAIN = RW_P + 3 * GD_W
T_AB = 0
T_IF = 128
T_MLQ = 256
T_MLK = 768
T_Z = 1280
T_MLV = 2816
T_MLO = 3840
P_TAIL = 4864


def _cparams(n_axes):
    return pltpu.CompilerParams(dimension_semantics=("arbitrary",) * n_axes, vmem_limit_bytes=VMEM_LIMIT)


def _bdot(a, b):
    return jnp.dot(a.astype(BF16), b.astype(BF16), preferred_element_type=F32)


def _bdot_nt(a, b):
    return lax.dot_general(a.astype(BF16), b.astype(BF16), (((1,), (1,)), ((), ())), preferred_element_type=F32)


def _bdot_tn(a, b):
    return lax.dot_general(a.astype(BF16), b.astype(BF16), (((0,), (0,)), ((), ())), preferred_element_type=F32)


def _split3(x):
    hi = x.astype(BF16)
    r1 = x - hi.astype(F32)
    mid = r1.astype(BF16)
    lo = (r1 - mid.astype(F32)).astype(BF16)
    return hi, mid, lo


def _cumsum_rows(x, tril_bf):
    hi, mid, lo = _split3(x)
    d = lambda t: jnp.dot(tril_bf, t, preferred_element_type=F32)
    return d(hi) + d(mid) + d(lo)


def _cols_to_rows(x, n_rows):
    sel = (lax.broadcasted_iota(jnp.int32, (n_rows, LANE), 0) == lax.broadcasted_iota(jnp.int32, (n_rows, LANE), 1)).astype(BF16)
    hi, mid, lo = _split3(x)
    d = lambda t: lax.dot_general(sel, t, (((1,), (1,)), ((), ())), preferred_element_type=F32)
    return d(hi) + d(mid) + d(lo)


def _head_selectors(width, hd):
    sh = int(math.log2(hd))
    e = (lax.broadcasted_iota(jnp.int32, (width, LANE), 0) >> sh) == lax.broadcasted_iota(jnp.int32, (width, LANE), 1)
    et = lax.broadcasted_iota(jnp.int32, (LANE, width), 0) == (lax.broadcasted_iota(jnp.int32, (LANE, width), 1) >> sh)
    return e.astype(BF16), et.astype(BF16)


def _head_sums(x, sel):
    e, et = sel
    hi, lo, _ = _split3(x)
    d = lambda t: jnp.dot(t, e, preferred_element_type=F32)
    hi, lo, _ = _split3(d(hi) + d(lo))
    d = lambda t: jnp.dot(t, et, preferred_element_type=F32)
    return d(hi) + d(lo)


def _tri_masks(C):
    ri = lax.broadcasted_iota(jnp.int32, (C, C), 0)
    ci = lax.broadcasted_iota(jnp.int32, (C, C), 1)
    return ri >= ci, ri > ci, ri == ci


def _unit_lower_inverse(As, eye, C):
    Ps = [-A for A in As]
    Ts = [jnp.where(eye, 1.0, P) for P in Ps]
    for _ in range(int(math.log2(C)) - 1):
        Ps = [_bdot(P, P) for P in Ps]
        Ts = [T + _bdot(T, P) for T, P in zip(Ts, Ps)]
        yield
    return Ts


def _interleave(gens):
    results, alive = [None] * len(gens), list(range(len(gens)))
    while alive:
        for idx in list(alive):
            try:
                next(gens[idx])
            except StopIteration as stop:
                results[idx] = stop.value
                alive.remove(idx)
    return results


def _sigmoid(x):
    return jax.nn.sigmoid(x)


def _softplus(x):
    return jnp.maximum(x, 0.0) + jnp.log(1.0 + jnp.exp(-jnp.abs(x)))


def _shift_rows(x, tail, s, C):
    xr = pltpu.roll(x, s, 0)
    r8 = lax.broadcasted_iota(jnp.int32, (TAIL, 1), 0)
    top = jnp.where(r8 < s, pltpu.roll(tail, s, 0), xr[0:TAIL])
    return jnp.concatenate([top, xr[TAIL:]], axis=0)


def _rmsnorm_body(x_ref, w_ref, o_ref):
    x = x_ref[...]
    y = x * lax.rsqrt(jnp.mean(x * x, -1, keepdims=True) + NORM_EPS)
    o_ref[...] = (y * w_ref[...]).astype(o_ref.dtype)


def _rmsnorm(x, w, out_dtype, rows, row_base, block):
    d = x.shape[1]
    rb = row_base // block
    return pl.pallas_call(
        _rmsnorm_body,
        grid=(rows // block,),
        in_specs=[pl.BlockSpec((block, d), lambda i: (rb + i, 0)), pl.BlockSpec((1, d), lambda i: (0, 0))],
        out_specs=pl.BlockSpec((block, d), lambda i: (i, 0)),
        out_shape=jax.ShapeDtypeStruct((rows, d), out_dtype),
        compiler_params=_cparams(1),
        name="rmsnorm",
    )(x, w.reshape(1, d))


def _lane_fold(x):
    return functools.reduce(lambda a, b: a + b, [x[:, c:c + LANE] for c in range(0, x.shape[1], LANE)])


def _row_scale(ssq_ref):
    return lax.rsqrt(jnp.sum(ssq_ref[...], -1, keepdims=True) * (1.0 / D_MODEL) + NORM_EPS)


def _prenorm_body(x_ref, w_ref, xw_ref, ssq_ref):
    x = x_ref[...]
    xw_ref[...] = (x * w_ref[...]).astype(xw_ref.dtype)
    ssq_ref[...] = _lane_fold(x * x)


def _prenorm(x, w, block):
    n, d = x.shape
    return pl.pallas_call(
        _prenorm_body, grid=(n // block,),
        in_specs=[pl.BlockSpec((block, d), lambda i: (i, 0)), pl.BlockSpec((1, d), lambda i: (0, 0))],
        out_specs=[pl.BlockSpec((block, d), lambda i: (i, 0)), pl.BlockSpec((block, LANE), lambda i: (i, 0))],
        out_shape=[jax.ShapeDtypeStruct((n, d), BF16), jax.ShapeDtypeStruct((n, LANE), F32)],
        compiler_params=_cparams(1), name="prenorm",
    )(x, w.reshape(1, d))


def _mm_scaled_body(a_ref, ssq_ref, w_ref, o_ref):
    o_ref[...] = _row_scale(ssq_ref) * jnp.dot(a_ref[...], w_ref[...].astype(BF16), preferred_element_type=F32)


def _mm_res_norm_body(a_ref, w_ref, r_ref, nw_ref, o_ref, xw_ref, ssq_ref):
    j = pl.program_id(1)
    acc = r_ref[...] + jnp.dot(a_ref[...], w_ref[...].astype(BF16), preferred_element_type=F32)
    o_ref[...] = acc
    xw_ref[...] = (acc * nw_ref[...]).astype(xw_ref.dtype)
    part = _lane_fold(acc * acc)

    @pl.when(j == 0)
    def _():
        ssq_ref[...] = part

    @pl.when(j > 0)
    def _():
        ssq_ref[...] += part


def _mm_res_body(a_ref, w_ref, r_ref, o_ref):
    o_ref[...] = r_ref[...] + jnp.dot(a_ref[...], w_ref[...].astype(BF16), preferred_element_type=F32)


def _matmul_wide(a, ssq, w, layer, name="matmul_wide"):
    m, k = a.shape
    n = w.shape[2]
    tm, tn = ROW_TILE_DOWN // 2, 4 * COL_TILE
    return pl.pallas_call(
        _mm_scaled_body, grid=(n // tn, m // tm),
        in_specs=[pl.BlockSpec((tm, k), lambda j, i: (i, 0)), pl.BlockSpec((tm, LANE), lambda j, i: (i, 0)),
                  pl.BlockSpec((None, k, tn), lambda j, i: (layer, 0, j))],
        out_specs=pl.BlockSpec((tm, tn), lambda j, i: (i, j)),
        out_shape=jax.ShapeDtypeStruct((m, n), F32),
        compiler_params=_cparams(2), name=name,
    )(a, ssq, w)


def _matmul(a, w, layer, tm, res, norm_w=None, name="matmul"):
    m, k = a.shape
    n = w.shape[2]
    a_spec = pl.BlockSpec((tm, k), lambda i, j: (i, 0), pipeline_mode=pl.Buffered(1))
    w_spec = pl.BlockSpec((None, k, COL_TILE), lambda i, j: (layer, 0, j))
    o_spec = pl.BlockSpec((tm, COL_TILE), lambda i, j: (i, j))
    if norm_w is None:
        return pl.pallas_call(
            _mm_res_body, grid=(m // tm, n // COL_TILE), in_specs=[a_spec, w_spec, o_spec], out_specs=o_spec,
            out_shape=jax.ShapeDtypeStruct((m, n), F32),
            compiler_params=_cparams(2), name=name,
        )(a, w, res)
    return pl.pallas_call(
        _mm_res_norm_body, grid=(m // tm, n // COL_TILE),
        in_specs=[a_spec, w_spec, o_spec, pl.BlockSpec((1, COL_TILE), lambda i, j: (0, j))],
        out_specs=[o_spec, o_spec, pl.BlockSpec((tm, LANE), lambda i, j: (i, 0))],
        out_shape=[jax.ShapeDtypeStruct((m, n), F32), jax.ShapeDtypeStruct((m, n), BF16),
                   jax.ShapeDtypeStruct((m, LANE), F32)],
        compiler_params=_cparams(2), name=name,
    )(a, w, res, norm_w.reshape(1, n))


def _row_spec(width, col_start, seg):
    row_base, C, B, nblk = seg
    rb, cb = row_base // C, col_start // width
    return pl.BlockSpec((C, width), lambda b, i: (rb + b * nblk + i, cb))


def _state_spec(shape):
    nd = len(shape)
    return pl.BlockSpec((1,) + tuple(shape[1:]), lambda b, i: (b,) + (0,) * (nd - 1))


def _param_spec(shape):
    nd = len(shape)
    return pl.BlockSpec(tuple(shape), lambda b, i: (0,) * nd)


def _seq_call(body, seg, row_ins, state_ins, params, buf, out_width, out_col, scratch, name, in_row_base=None):
    row_base, C, B, nblk = seg
    in_seg = seg if in_row_base is None else (in_row_base, C, B, nblk)
    in_specs = [_row_spec(w, c, in_seg) for (_, w, c) in row_ins]
    in_specs += [_state_spec(s.shape) for s in state_ins]
    in_specs += [_param_spec(p.shape) for p in params]
    in_specs += [pl.BlockSpec(memory_space=pl.ANY)]
    args = [a for (a, _, _) in row_ins] + list(state_ins) + list(params) + [buf]
    out_specs = [_row_spec(out_width, out_col, seg)] + [_state_spec(s.shape) for s in state_ins]
    out_shape = [jax.ShapeDtypeStruct(buf.shape, buf.dtype)] + [jax.ShapeDtypeStruct(s.shape, s.dtype) for s in state_ins]
    return pl.pallas_call(
        functools.partial(body, C, nblk), grid=(B, nblk), in_specs=in_specs, out_specs=out_specs,
        out_shape=out_shape, scratch_shapes=scratch,
        input_output_aliases={len(args) - 1: 0},
        compiler_params=_cparams(2), name=name,
    )(*args)


def _rwkv_main(C, p, carry, s_scr, prm):
    mu, w0, wup, a0, aup, gup, kkw, kaw, rkw, lnw, lnb = prm
    rows = lax.broadcasted_iota(jnp.int32, (C, 1), 0)
    prev = jnp.where(rows == 0, carry[...], pltpu.roll(p, 1, 0))
    carry[...] = p[C - 1:C, :]
    xs = p + (prev - p) * mu[...]
    r, k, v = xs[:, 0:RW_W], xs[:, RW_W:2 * RW_W], xs[:, 2 * RW_W:3 * RW_W]
    dw, da, dg = xs[:, 3 * RW_W:3 * RW_W + 64], xs[:, 3 * RW_W + 64:3 * RW_W + 128], xs[:, 3 * RW_W + 128:RW_P]
    lw = -math.exp(-0.5) * _sigmoid(w0[...] + _bdot(jnp.tanh(dw), wup[...]))
    a = _sigmoid(a0[...] + _bdot(da, aup[...]))
    g = _bdot(_sigmoid(dg), gup[...])
    sel = _head_selectors(RW_W, RW_HD)
    kkx = k * kkw[...]
    kkn = kkx * lax.rsqrt(_head_sums(kkx * kkx, sel) + 1e-6)
    kt = k * (1.0 + (a - 1.0) * kaw[...])
    bonus = _head_sums(r * kt * rkw[...], sel) * v
    kb = kkn * a
    yield

    tril, strict, eye = _tri_masks(C)
    cl = _cumsum_rows(lw, tril.astype(BF16))
    w_in = jnp.exp(cl)
    w_ex = jnp.exp(cl - lw)
    w_inv = jnp.exp(-cl)
    w_end = jnp.exp(cl[C - 1:C, :] - cl)

    heads = range(RW_HEADS)
    sls = [slice(RW_HD * h, RW_HD * (h + 1)) for h in heads]
    kk_ex, r_in = kkn * w_ex, r * w_in
    kt_inv, kb_inv = kt * w_inv, kb * w_inv
    kt_end, kb_end = kt * w_end, kb * w_end
    yield
    lhss = [jnp.concatenate([kk_ex[:, sl], r_in[:, sl]], axis=0) for sl in sls]
    rhss = [jnp.concatenate([kt_inv[:, sl], kb_inv[:, sl]], axis=0) for sl in sls]
    scs = [_bdot_nt(lhs, rhs) for lhs, rhs in zip(lhss, rhss)]
    yield
    s0s = [s_scr[h] for h in heads]
    pss = [_bdot_nt(lhs, s0) for lhs, s0 in zip(lhss, s0s)]
    yield
    negs = [-(ps[:C] + _bdot(jnp.where(strict, sc[:C, :C], 0.0), v[:, sl])) for ps, sc, sl in zip(pss, scs, sls)]
    yield
    invs = yield from _unit_lower_inverse([jnp.where(strict, sc[:C, C:], 0.0) for sc in scs], eye, C)
    us = [_bdot(t, n) for t, n in zip(invs, negs)]
    yield
    vus = [jnp.concatenate([v[:, sl], u], axis=0) for u, sl in zip(us, sls)]
    a_rs = [jnp.concatenate([jnp.where(tril, sc[C:, :C], 0.0), jnp.where(tril, sc[C:, C:], 0.0)], axis=1) for sc in scs]
    ys = [ps[C:] + _bdot(a_r, vu) for ps, a_r, vu in zip(pss, a_rs, vus)]
    yield
    kes = [jnp.concatenate([kt_end[:, sl], kb_end[:, sl]], axis=0) for sl in sls]
    for h in heads:
        s_scr[h] = s0s[h] * w_in[C - 1:C, sls[h]] + _bdot_tn(vus[h], kes[h])
    yield
    y = jnp.concatenate(ys, axis=1)
    yc = y - _head_sums(y, sel) * (1.0 / RW_HD)
    yn = yc * lax.rsqrt(_head_sums(yc * yc, sel) * (1.0 / RW_HD) + RW_GN_EPS)
    return (yn * lnw[...] + lnb[...] + bonus) * g


def _gdn_main(C, x, z, ab, tail, s_scr, prm):
    convw, alog, dtb, normw = prm
    t8 = tail[...]
    conv = x * convw[3:4, :]
    for s in (1, 2, 3):
        conv = conv + _shift_rows(x, t8, s, C) * convw[3 - s:4 - s, :]
    tail[...] = x[C - TAIL:, :]
    act = conv * _sigmoid(conv)
    q, k, v = act[:, 0:GD_W], act[:, GD_W:2 * GD_W], act[:, 2 * GD_W:3 * GD_W]
    yield

    g_all = -jnp.exp(alog[...]) * _softplus(ab + dtb[...])
    beta_all = _sigmoid(ab)
    tril, strict, eye = _tri_masks(C)
    gc = _cumsum_rows(g_all, tril.astype(BF16))
    gr = _cols_to_rows(gc, 16)
    eg = jnp.exp(gc)
    yield

    heads = range(GD_HEADS)
    sls = [slice(GD_HD * h, GD_HD * (h + 1)) for h in heads]
    qs = [q[:, sl] for sl in sls]
    qs = [t * (lax.rsqrt(jnp.sum(t * t, -1, keepdims=True) + 1e-6) * GD_HD ** -0.5) for t in qs]
    ks = [k[:, sl] for sl in sls]
    ks = [t * lax.rsqrt(jnp.sum(t * t, -1, keepdims=True) + 1e-6) for t in ks]
    yield
    betas = [beta_all[:, GD_HEADS + h:GD_HEADS + h + 1] for h in heads]
    decays = [jnp.where(tril, jnp.exp(jnp.where(tril, gc[:, h:h + 1] - gr[h:h + 1, :], 0.0)), 0.0) for h in heads]
    yield
    scs = [_bdot_nt(jnp.concatenate([k_h, q_h], axis=0), k_h) for k_h, q_h in zip(ks, qs)]
    yield
    invs = yield from _unit_lower_inverse(
        [jnp.where(strict, beta * sc[:C] * decay, 0.0) for beta, sc, decay in zip(betas, scs, decays)], eye, C)
    rhss = [jnp.concatenate([v[:, sl] * beta, k_h * (beta * eg[:, h:h + 1])], axis=1)
            for h, sl, beta, k_h in zip(heads, sls, betas, ks)]
    sols = [_bdot(t, rhs) for t, rhs in zip(invs, rhss)]
    yield
    s0s = [s_scr[h] for h in heads]
    pss = [_bdot_nt(jnp.concatenate([sol[:, GD_HD:], q_h], axis=0), s0) for sol, q_h, s0 in zip(sols, qs, s0s)]
    yield
    us = [sol[:, :GD_HD] - ps[:C] for sol, ps in zip(sols, pss)]
    os_ = [eg[:, h:h + 1] * ps[C:] + _bdot(sc[C:] * decay, u) for h, ps, sc, decay, u in zip(heads, pss, scs, decays, us)]
    yield
    for h in heads:
        gl = gc[C - 1:C, h:h + 1]
        s_scr[h] = jnp.exp(gl) * s0s[h] + _bdot_tn(us[h] * jnp.exp(gl - gc[:, h:h + 1]), ks[h])
    yield
    outs = [o * lax.rsqrt(jnp.mean(o * o, -1, keepdims=True) + NORM_EPS) * normw[...] for o in os_]
    return jnp.concatenate(outs, axis=1) * (z * _sigmoid(z))


def _mlstm_main(C, q, k, v, og, gates, c_scr, n_scr, m_scr, prm):
    ib, fb, normw = prm
    li_all = gates + ib[...]
    x = gates + fb[...]
    lf_all = jnp.minimum(x, 0.0) - jnp.log(1.0 + jnp.exp(-jnp.abs(x)))
    tril, _, _ = _tri_masks(C)
    fc = _cumsum_rows(lf_all, tril.astype(BF16))
    fr = _cols_to_rows(fc, 8)
    lir = _cols_to_rows(li_all, 8)
    lane = lax.broadcasted_iota(jnp.int32, (1, LANE), 1)
    m_all = m_scr[...]
    m_new = m_all
    yield

    heads = range(ML_HEADS)
    qs = [q[:, ML_DK * h:ML_DK * (h + 1)] * ML_DK ** -0.5 for h in heads]
    ks = [k[:, ML_DK * h:ML_DK * (h + 1)] for h in heads]
    vs = [v[:, ML_DV * h:ML_DV * (h + 1)] for h in heads]
    c0s = [c_scr[h] for h in heads]
    n0s = [n_scr[h:h + 1, :] for h in heads]
    qks = [_bdot_nt(q_h, k_h) for q_h, k_h in zip(qs, ks)]
    qcs = [_bdot_nt(q_h, c0) for q_h, c0 in zip(qs, c0s)]
    yield
    fcs = [fc[:, ML_HEADS + h:ML_HEADS + h + 1] for h in heads]
    m0s = [m_all[:, h:h + 1] for h in heads]
    dmats = [jnp.where(tril, fc_h - fr[ML_HEADS + h:ML_HEADS + h + 1, :] + lir[h:h + 1, :], -jnp.inf)
             for h, fc_h in zip(heads, fcs)]
    inters = [fc_h + m0 for fc_h, m0 in zip(fcs, m0s)]
    ms = [jnp.maximum(inter, jnp.max(dmat, -1, keepdims=True)) for inter, dmat in zip(inters, dmats)]
    yield
    ss = [qk * jnp.exp(dmat - m) for qk, dmat, m in zip(qks, dmats, ms)]
    wis = [jnp.exp(inter - m) for inter, m in zip(inters, ms)]
    yield
    nums = [wi * qc + _bdot(s, v_h) for wi, qc, s, v_h in zip(wis, qcs, ss, vs)]
    dens = [wi * jnp.sum(q_h * n0, -1, keepdims=True) + jnp.sum(s, -1, keepdims=True)
            for wi, q_h, n0, s in zip(wis, qs, n0s, ss)]
    yield
    hhs = [num / jnp.maximum(jnp.abs(den), jnp.exp(-m)) for num, den, m in zip(nums, dens, ms)]
    m_cs = [m[C - 1:C, :] for m in ms]
    wends = [jnp.exp(fc_h[C - 1:C, :] - fc_h + li_all[:, h:h + 1] - m_c) for h, fc_h, m_c in zip(heads, fcs, m_cs)]
    dstates = [jnp.exp(fc_h[C - 1:C, :] + m0 - m_c) for fc_h, m0, m_c in zip(fcs, m0s, m_cs)]
    for h in heads:
        c_scr[h] = dstates[h] * c0s[h] + _bdot_tn(vs[h] * wends[h], ks[h])
        n_scr[h:h + 1, :] = dstates[h] * n0s[h] + jnp.sum(wends[h] * ks[h], 0, keepdims=True)
        m_new = jnp.where(lane == h, m_cs[h], m_new)
    m_scr[...] = m_new
    yield
    outs = [hh * lax.rsqrt(jnp.mean(hh * hh, -1, keepdims=True) + NORM_EPS) for hh in hhs]
    return jnp.concatenate(outs, axis=1) * normw[...] * _sigmoid(og)


N_MIX_STATES = 7
N_RW_PRM, N_GD_PRM, N_ML_PRM = 11, 4, 3


def _mixer_body(C, nblk, p_ref, *refs):
    ins, refs = refs[:N_MIX_STATES], refs[N_MIX_STATES:]
    rw_prm, refs = refs[:N_RW_PRM], refs[N_RW_PRM:]
    gd_prm, refs = refs[:N_GD_PRM], refs[N_GD_PRM:]
    ml_prm, refs = refs[:N_ML_PRM], refs[N_ML_PRM:]
    o_ref, refs = refs[1], refs[2:]
    outs, scr = refs[:N_MIX_STATES], refs[N_MIX_STATES:]
    carry, s_rw, tail, s_gd, c_scr, n_scr, m_scr = scr
    i = pl.program_id(1)

    @pl.when(i == 0)
    def _():
        for dst, src in zip(scr, ins):
            dst[...] = src[0]

    pm, pt = p_ref[:, 0:P_MAIN], p_ref[:, P_MAIN:P_MAIN + P_TAIL]
    o_rw, o_gd, o_ml = _interleave([
        _rwkv_main(C, pm[:, 0:RW_P], carry, s_rw, rw_prm),
        _gdn_main(C, pm[:, RW_P:P_MAIN], pt[:, T_Z:T_Z + GD_W], pt[:, T_AB:T_AB + LANE], tail, s_gd, gd_prm),
        _mlstm_main(C, pt[:, T_MLQ:T_MLQ + ML_QK], pt[:, T_MLK:T_MLK + ML_QK], pt[:, T_MLV:T_MLV + ML_W],
                    pt[:, T_MLO:T_MLO + ML_W], pt[:, T_IF:T_IF + LANE], c_scr, n_scr, m_scr, ml_prm)])
    o_ref[:, 0:RW_W] = o_rw.astype(o_ref.dtype)
    o_ref[:, RW_W:RW_W + GD_W] = o_gd.astype(o_ref.dtype)
    o_ref[:, RW_W + GD_W:] = o_ml.astype(o_ref.dtype)

    @pl.when(i == nblk - 1)
    def _():
        outs[0][0] = pm[C - 1:C, 0:RW_P]
        for dst, src in zip(outs[1:], scr[1:]):
            dst[0] = src[...]


def _mixer_call(p, buf, seg, states, params):
    scratch = [pltpu.VMEM((1, RW_P), F32), pltpu.VMEM((RW_HEADS, RW_HD, RW_HD), F32),
               pltpu.VMEM((TAIL, 3 * GD_W), F32), pltpu.VMEM((GD_HEADS, GD_HD, GD_HD), F32),
               pltpu.VMEM((ML_HEADS, ML_DV, ML_DK), F32), pltpu.VMEM((ML_HEADS, ML_DK), F32), pltpu.VMEM((1, LANE), F32)]
    outs = _seq_call(_mixer_body, seg, [(p, P_MAIN + P_TAIL, 0)], list(states), params,
                     buf, D_MODEL, 0, scratch, "mixers")
    return outs[0], list(outs[1:])


def _ffn_act_body(C, nblk, g_ref, u_ref, tail_in, convw, convb, _buf, o_ref, tail_out, tail):
    i = pl.program_id(1)

    @pl.when(i == 0)
    def _():
        tail[...] = tail_in[0]

    x = g_ref[...]
    t8 = tail[...]
    conv = x * convw[2:3, :] + convb[...]
    for s in (1, 2):
        conv = conv + _shift_rows(x, t8, s, C) * convw[2 - s:3 - s, :]
    tail[...] = x[C - TAIL:, :]
    o_ref[...] = (conv * _sigmoid(conv) * u_ref[...]).astype(o_ref.dtype)

    @pl.when(i == nblk - 1)
    def _():
        tail_out[0] = tail[...]


def _ffn_act_call(g, u, buf, seg, tail, params, in_row_base=None):
    outs = _seq_call(
        _ffn_act_body, seg, [(g, D_FF, 0), (u, D_FF, 0)], [tail], params, buf, D_FF, 0,
        [pltpu.VMEM((TAIL, D_FF), F32)], "ffn_act", in_row_base)
    return outs[0], outs[1]


def _ffn_up_body(tm, h_ref, ssq_ref, wg_ref, wu_ref, convw, convb, o_ref, gt_ref, ut_ref, carry):
    i, j = pl.program_id(0), pl.program_id(1)
    h = h_ref[...]
    scale = _row_scale(ssq_ref)
    g = scale * jnp.dot(h, wg_ref[...].astype(BF16), preferred_element_type=F32)
    u = scale * jnp.dot(h, wu_ref[...].astype(BF16), preferred_element_type=F32)
    gt_ref[...] = g[tm - FFN_TAIL_ROWS:, :]
    ut_ref[...] = u[tm - FFN_TAIL_ROWS:, :]
    rows = i * tm + lax.broadcasted_iota(jnp.int32, (tm, 1), 0)
    g = jnp.where(rows >= ROW_META, g, 0.0)

    @pl.when(i == 0)
    def _():
        carry[j] = jnp.zeros((TAIL, COL_TILE), F32)

    t8 = carry[j]
    conv = g * convw[2:3, :] + convb[...]
    for s in (1, 2):
        conv = conv + _shift_rows(g, t8, s, tm) * convw[2 - s:3 - s, :]
    carry[j] = g[tm - TAIL:, :]
    o_ref[...] = (conv * _sigmoid(conv) * u).astype(o_ref.dtype)


def _ffn_up(hb, ssq, w_gate, w_up, layer, convw, convb, tm):
    m, k = hb.shape
    n = w_gate.shape[2]
    w_spec = pl.BlockSpec((None, k, COL_TILE), lambda i, j: (layer, 0, j))
    t_spec = pl.BlockSpec((FFN_TAIL_ROWS, COL_TILE), lambda i, j: (i, j))
    return pl.pallas_call(
        functools.partial(_ffn_up_body, tm), grid=(m // tm, n // COL_TILE),
        in_specs=[pl.BlockSpec((tm, k), lambda i, j: (i, 0), pipeline_mode=pl.Buffered(1)),
                  pl.BlockSpec((tm, LANE), lambda i, j: (i, 0)), w_spec, w_spec,
                  pl.BlockSpec((3, COL_TILE), lambda i, j: (0, j)), pl.BlockSpec((1, COL_TILE), lambda i, j: (0, j))],
        out_specs=[pl.BlockSpec((tm, COL_TILE), lambda i, j: (i, j)), t_spec, t_spec],
        out_shape=[jax.ShapeDtypeStruct((m, n), BF16), jax.ShapeDtypeStruct((m // tm * FFN_TAIL_ROWS, n), F32),
                   jax.ShapeDtypeStruct((m // tm * FFN_TAIL_ROWS, n), F32)],
        scratch_shapes=[pltpu.VMEM((n // COL_TILE, TAIL, COL_TILE), F32)],
        compiler_params=_cparams(2), name="ffn_up_act",
    )(hb, ssq, w_gate, w_up, convw, convb)


def _pad_cols(x, width):
    return jnp.pad(x, ((0, 0),) * (x.ndim - 1) + ((0, width - x.shape[-1]),))


def _lane_vec(x, start=0):
    return jnp.zeros((1, LANE), F32).at[0, start:start + x.shape[0]].set(x)


def _tail_cols(w):
    ml0 = RW_P + GD_P
    gd_ab = w[..., P_MAIN:P_MAIN + 2 * GD_HEADS]
    ml_if = w[..., ml0 + 2 * ML_QK + ML_W:ml0 + 2 * ML_QK + ML_W + 2 * ML_HEADS]
    cols = [_pad_cols(gd_ab, LANE), _pad_cols(ml_if, LANE), w[..., ml0:ml0 + 2 * ML_QK],
            w[..., P_MAIN + 2 * GD_HEADS:ml0], w[..., ml0 + 2 * ML_QK:ml0 + 2 * ML_QK + ML_W],
            w[..., ml0 + 2 * ML_QK + ML_W + 2 * ML_HEADS:ml0 + ML_P]]
    return jnp.concatenate(cols, axis=-1)


def _pad_tail(buf, width):
    b, r, w = buf.shape
    return jnp.pad(buf, ((0, 0), (TAIL - r, 0), (0, width - w)))


def _layer(x, xw, ssq, states, lp, wts, layer, segs):
    n = x.shape[0]
    p = _matmul_wide(xw, ssq, wts["w_in"], layer, name="proj_in")
    mix = jnp.zeros((n, D_MODEL), BF16)
    new_states = []
    for seg, (mix_st, ffn_tail) in zip(segs, states):
        if mix_st is None:
            mix_st = new_states[-1][0]
        mix, mix_new = _mixer_call(p, mix, seg, mix_st, lp["mixers"])
        new_states.append([mix_new, ffn_tail])
    x, xw, ssq = _matmul(mix, wts["w_out"], layer, ROW_TILE, x, norm_w=lp["norm_ffn_w"], name="proj_out")
    act, g_tail, u_tail = _ffn_up(xw, ssq, wts["w_gate"], wts["w_up"], layer, lp["ffn"][0], lp["ffn"][1], ROW_TILE)
    tail_base = n - g_tail.shape[0]
    for idx, seg in enumerate(segs):
        if idx == 0 or states[idx][0] is None:
            end = seg[0] + seg[1] * seg[2] * seg[3]
            kept = end - TAIL >= n - FFN_TAIL_ROWS
            new_states[idx][1] = g_tail[end - TAIL - tail_base:end - tail_base][None] if kept else None
        else:
            assert seg[0] >= n - FFN_TAIL_ROWS
            act, new_states[idx][1] = _ffn_act_call(g_tail, u_tail, act, seg, new_states[idx][1], lp["ffn"],
                                                    in_row_base=seg[0] - tail_base)
    if lp["next_norm_w"] is None:
        return _matmul(act, wts["w_down"], layer, ROW_TILE_DOWN, x, name="ffn_down"), None, None, new_states
    x, xw, ssq = _matmul(act, wts["w_down"], layer, ROW_TILE_DOWN, x, norm_w=lp["next_norm_w"], name="ffn_down")
    return x, xw, ssq, new_states


def kernel(x_prompt, x_sample, state_rwkv_wkv, state_rwkv_shift, state_gdn, cache_gdn_conv, state_mlstm_c, state_mlstm_n, state_mlstm_m, cache_ffn_conv, meta_tokens, norm_mix_w, w_in, rwkv_mu, rwkv_w0, rwkv_w_up, rwkv_a0, rwkv_a_up, rwkv_g_up, rwkv_k_k, rwkv_k_a, rwkv_r_k, rwkv_ln_w, rwkv_ln_b, gdn_conv_w, gdn_a_log, gdn_dt_bias, gdn_norm_w, mlstm_i_b, mlstm_f_b, mlstm_norm_w, w_out, norm_ffn_w, ffn_w_gate, ffn_w_up, ffn_conv_w, ffn_conv_b, ffn_w_down, final_norm_w):
    depth = w_in.shape[0]
    seq = x_prompt.shape[1]
    dec_b, dec_seq = x_sample.shape[0], x_sample.shape[1]
    row_sample = ROW_PROMPT + seq
    n_rows = row_sample + dec_b * dec_seq
    n_pad = -(-n_rows // ROW_TILE) * ROW_TILE
    segs = ((ROW_META, N_META, 1, 1), (ROW_PROMPT, CHUNK, 1, seq // CHUNK), (row_sample, dec_seq, dec_b, 1))

    x = jnp.concatenate([
        jnp.zeros((ROW_META, D_MODEL), F32), meta_tokens.astype(F32), x_prompt[0],
        x_sample.reshape(dec_b * dec_seq, D_MODEL), jnp.zeros((n_pad - n_rows, D_MODEL), F32)], axis=0)

    row = lambda a: a.reshape(1, -1)
    zero_mix = [jnp.zeros((1, 1, RW_P), F32), jnp.zeros((1, RW_HEADS, RW_HD, RW_HD), F32),
                jnp.zeros((1, TAIL, 3 * GD_W), F32), jnp.zeros((1, GD_HEADS, GD_HD, GD_HD), F32),
                jnp.zeros((1, ML_HEADS, ML_DV, ML_DK), F32), jnp.zeros((1, ML_HEADS, ML_DK), F32),
                jnp.zeros((1, 1, LANE), F32)]

    wts = {
        "w_in": jnp.concatenate([w_in[..., :P_MAIN], _tail_cols(w_in)], axis=-1).astype(BF16),
        "w_out": w_out,
        "w_gate": ffn_w_gate,
        "w_up": ffn_w_up,
        "w_down": ffn_w_down.astype(BF16),
    }

    xw, ssq = _prenorm(x, norm_mix_w[0], ROW_TILE_SMALL)
    p_out, s_out = [], []
    for l in range(depth):
        lp = {
            "next_norm_w": norm_mix_w[l + 1] if l + 1 < depth else None,
            "mixers": [row(rwkv_mu[l]), row(rwkv_w0[l]), rwkv_w_up[l], row(rwkv_a0[l]), rwkv_a_up[l], rwkv_g_up[l],
                       row(rwkv_k_k[l]), row(rwkv_k_a[l]), row(rwkv_r_k[l]), row(rwkv_ln_w[l]), row(rwkv_ln_b[l]),
                       gdn_conv_w[l], _lane_vec(gdn_a_log[l]), _lane_vec(gdn_dt_bias[l]), row(gdn_norm_w[l]),
                       _lane_vec(mlstm_i_b[l]), _lane_vec(mlstm_f_b[l], ML_HEADS), row(mlstm_norm_w[l])],
            "norm_ffn_w": norm_ffn_w[l],
            "ffn": [ffn_conv_w[l], row(ffn_conv_b[l])],
        }
        sample_mix = [state_rwkv_shift[l][:, None, :], state_rwkv_wkv[l],
                      _pad_tail(cache_gdn_conv[l], 3 * GD_W), state_gdn[l],
                      state_mlstm_c[l], state_mlstm_n[l], _pad_cols(state_mlstm_m[l], LANE)[:, None, :]]
        states = ([zero_mix, None], [None, None], [sample_mix, _pad_tail(cache_ffn_conv[l], D_FF)])
        x, xw, ssq, st = _layer(x, xw, ssq, states, lp, wts, l, segs)
        p_out.append(st[1])
        s_out.append(st[2])

    y_prompt = _rmsnorm(x, final_norm_w, F32, seq, ROW_PROMPT, CHUNK).reshape(1, seq, D_MODEL)
    y_sample = _rmsnorm(x, final_norm_w, F32, dec_b * dec_seq, row_sample, CHUNK).reshape(dec_b, dec_seq, D_MODEL)

    def collect(sts):
        stack = lambda f: jnp.stack([f(st) for st in sts], 0)
        return (stack(lambda st: st[0][1]), stack(lambda st: st[0][0][:, 0, :]),
                stack(lambda st: st[0][3]), stack(lambda st: st[0][2][:, TAIL - 3:, :]),
                stack(lambda st: st[0][4]), stack(lambda st: st[0][5]), stack(lambda st: st[0][6][:, 0, :ML_HEADS]),
                stack(lambda st: st[1][:, TAIL - 2:, :]))

    return (y_prompt, y_sample) + collect(p_out) + collect(s_out)
```

```python
import functools
import math

import jax
import jax.numpy as jnp
from jax import lax
from jax.experimental import pallas as pl
from jax.experimental.pallas import tpu as pltpu

F32 = jnp.float32
BF16 = jnp.bfloat16

D_MODEL = 4096
N_META = 16
CHUNK = 64
NORM_EPS = 1e-6
RW_HEADS, RW_HD = 24, 64
RW_W = RW_HEADS * RW_HD
RW_P = 3 * RW_W + 64 + 64 + 128
RW_GN_EPS = 64e-5
GD_HEADS, GD_HD = 12, 128
GD_W = GD_HEADS * GD_HD
GD_P = 4 * GD_W + 2 * GD_HEADS
ML_HEADS, ML_DK, ML_DV = 4, 128, 256
ML_W = ML_HEADS * ML_DV
ML_QK = ML_HEADS * ML_DK
ML_P = 2 * ML_QK + 2 * ML_W + 2 * ML_HEADS
D_FF = 11008

LANE = 128
TAIL = 8
ROW_META = 48
ROW_PROMPT = 64
ROW_TILE = 2128
ROW_TILE_DOWN = 1216
ROW_TILE_SMALL = 448
COL_TILE = 256
FFN_TAIL_ROWS = 512
VMEM_LIMIT = 56 * 1024 * 1024

P_MAIN = RW_P + 3 * GD_W
T_AB = 0
T_IF = 128
T_MLQ = 256
T_MLK = 768
T_Z = 1280
T_MLV = 2816
T_MLO = 3840
P_TAIL = 4864


def _cparams(n_axes):
    return pltpu.CompilerParams(dimension_semantics=("arbitrary",) * n_axes, vmem_limit_bytes=VMEM_LIMIT)


def _bdot(a, b):
    return jnp.dot(a.astype(BF16), b.astype(BF16), preferred_element_type=F32)


def _bdot_nt(a, b):
    return lax.dot_general(a.astype(BF16), b.astype(BF16), (((1,), (1,)), ((), ())), preferred_element_type=F32)


def _bdot_tn(a, b):
    return lax.dot_general(a.astype(BF16), b.astype(BF16), (((0,), (0,)), ((), ())), preferred_element_type=F32)


def _split3(x):
    hi = x.astype(BF16)
    r1 = x - hi.astype(F32)
    mid = r1.astype(BF16)
    lo = (r1 - mid.astype(F32)).astype(BF16)
    return hi, mid, lo


def _cumsum_rows(x, tril_bf):
    hi, mid, lo = _split3(x)
    d = lambda t: jnp.dot(tril_bf, t, preferred_element_type=F32)
    return d(hi) + d(mid) + d(lo)


def _cols_to_rows(x, n_rows):
    sel = (lax.broadcasted_iota(jnp.int32, (n_rows, LANE), 0) == lax.broadcasted_iota(jnp.int32, (n_rows, LANE), 1)).astype(BF16)
    hi, mid, lo = _split3(x)
    d = lambda t: lax.dot_general(sel, t, (((1,), (1,)), ((), ())), preferred_element_type=F32)
    return d(hi) + d(mid) + d(lo)


def _head_selectors(width, hd):
    sh = int(math.log2(hd))
    e = (lax.broadcasted_iota(jnp.int32, (width, LANE), 0) >> sh) == lax.broadcasted_iota(jnp.int32, (width, LANE), 1)
    et = lax.broadcasted_iota(jnp.int32, (LANE, width), 0) == (lax.broadcasted_iota(jnp.int32, (LANE, width), 1) >> sh)
    return e.astype(BF16), et.astype(BF16)


def _head_sums(x, sel):
    e, et = sel
    hi, lo, _ = _split3(x)
    d = lambda t: jnp.dot(t, e, preferred_element_type=F32)
    hi, lo, _ = _split3(d(hi) + d(lo))
    d = lambda t: jnp.dot(t, et, preferred_element_type=F32)
    return d(hi) + d(lo)


def _tri_masks(C):
    ri = lax.broadcasted_iota(jnp.int32, (C, C), 0)
    ci = lax.broadcasted_iota(jnp.int32, (C, C), 1)
    return ri >= ci, ri > ci, ri == ci


def _unit_lower_inverse(As, eye, C):
    Ps = [-A for A in As]
    Ts = [jnp.where(eye, 1.0, P) for P in Ps]
    levels = int(math.log2(C))
    if levels > 1:
        Ps = [_bdot(P, P) for P in Ps]
        yield
    for k in range(1, levels):
        if k < levels - 1:
            Ys = [_bdot(jnp.concatenate([T, P], axis=0), P) for T, P in zip(Ts, Ps)]
            Ts = [T + Y[:C] for T, Y in zip(Ts, Ys)]
            Ps = [Y[C:] for Y in Ys]
        else:
            Ts = [T + _bdot(T, P) for T, P in zip(Ts, Ps)]
        yield
    return Ts


def _interleave(gens):
    results, alive = [None] * len(gens), list(range(len(gens)))
    while alive:
        for idx in list(alive):
            try:
                next(gens[idx])
            except StopIteration as stop:
                results[idx] = stop.value
                alive.remove(idx)
    return results


def _sigmoid(x):
    return jax.nn.sigmoid(x)


def _softplus(x):
    return jnp.maximum(x, 0.0) + jnp.log(1.0 + jnp.exp(-jnp.abs(x)))


def _shift_rows(x, tail, s, C):
    xr = pltpu.roll(x, s, 0)
    r8 = lax.broadcasted_iota(jnp.int32, (TAIL, 1), 0)
    top = jnp.where(r8 < s, pltpu.roll(tail, s, 0), xr[0:TAIL])
    return jnp.concatenate([top, xr[TAIL:]], axis=0)


def _rmsnorm_body(x_ref, w_ref, o_ref):
    x = x_ref[...]
    y = x * lax.rsqrt(jnp.mean(x * x, -1, keepdims=True) + NORM_EPS)
    o_ref[...] = (y * w_ref[...]).astype(o_ref.dtype)


def _rmsnorm(x, w, out_dtype, rows, row_base, block):
    d = x.shape[1]
    rb = row_base // block
    return pl.pallas_call(
        _rmsnorm_body,
        grid=(rows // block,),
        in_specs=[pl.BlockSpec((block, d), lambda i: (rb + i, 0)), pl.BlockSpec((1, d), lambda i: (0, 0))],
        out_specs=pl.BlockSpec((block, d), lambda i: (i, 0)),
        out_shape=jax.ShapeDtypeStruct((rows, d), out_dtype),
        compiler_params=_cparams(1),
        name="rmsnorm",
    )(x, w.reshape(1, d))


def _lane_fold(x):
    return functools.reduce(lambda a, b: a + b, [x[:, c:c + LANE] for c in range(0, x.shape[1], LANE)])


def _row_scale(ssq_ref):
    return lax.rsqrt(jnp.sum(ssq_ref[...], -1, keepdims=True) * (1.0 / D_MODEL) + NORM_EPS)


def _prenorm_body(x_ref, w_ref, xw_ref, ssq_ref):
    x = x_ref[...]
    xw_ref[...] = (x * w_ref[...]).astype(xw_ref.dtype)
    ssq_ref[...] = _lane_fold(x * x)


def _prenorm(x, w, block):
    n, d = x.shape
    return pl.pallas_call(
        _prenorm_body, grid=(n // block,),
        in_specs=[pl.BlockSpec((block, d), lambda i: (i, 0)), pl.BlockSpec((1, d), lambda i: (0, 0))],
        out_specs=[pl.BlockSpec((block, d), lambda i: (i, 0)), pl.BlockSpec((block, LANE), lambda i: (i, 0))],
        out_shape=[jax.ShapeDtypeStruct((n, d), BF16), jax.ShapeDtypeStruct((n, LANE), F32)],
        compiler_params=_cparams(1), name="prenorm",
    )(x, w.reshape(1, d))


def _mm_scaled_body(a_ref, ssq_ref, w_ref, o_ref):
    o_ref[...] = _row_scale(ssq_ref) * jnp.dot(a_ref[...], w_ref[...].astype(BF16), preferred_element_type=F32)


def _mm_res_norm_body(a_ref, w_ref, r_ref, nw_ref, o_ref, xw_ref, ssq_ref):
    j = pl.program_id(1)
    acc = r_ref[...] + jnp.dot(a_ref[...], w_ref[...].astype(BF16), preferred_element_type=F32)
    o_ref[...] = acc
    xw_ref[...] = (acc * nw_ref[...]).astype(xw_ref.dtype)
    part = _lane_fold(acc * acc)

    @pl.when(j == 0)
    def _():
        ssq_ref[...] = part

    @pl.when(j > 0)
    def _():
        ssq_ref[...] += part


def _mm_res_body(a_ref, w_ref, r_ref, o_ref):
    o_ref[...] = r_ref[...] + jnp.dot(a_ref[...], w_ref[...].astype(BF16), preferred_element_type=F32)


def _matmul_wide(a, ssq, w, layer, name="matmul_wide"):
    m, k = a.shape
    n = w.shape[2]
    tm, tn = ROW_TILE_DOWN // 2, 4 * COL_TILE
    return pl.pallas_call(
        _mm_scaled_body, grid=(n // tn, m // tm),
        in_specs=[pl.BlockSpec((tm, k), lambda j, i: (i, 0)), pl.BlockSpec((tm, LANE), lambda j, i: (i, 0)),
                  pl.BlockSpec((None, k, tn), lambda j, i: (layer, 0, j))],
        out_specs=pl.BlockSpec((tm, tn), lambda j, i: (i, j)),
        out_shape=jax.ShapeDtypeStruct((m, n), F32),
        compiler_params=_cparams(2), name=name,
    )(a, ssq, w)


def _matmul(a, w, layer, tm, res, norm_w=None, name="matmul"):
    m, k = a.shape
    n = w.shape[2]
    a_spec = pl.BlockSpec((tm, k), lambda i, j: (i, 0), pipeline_mode=pl.Buffered(1))
    w_spec = pl.BlockSpec((None, k, COL_TILE), lambda i, j: (layer, 0, j))
    o_spec = pl.BlockSpec((tm, COL_TILE), lambda i, j: (i, j))
    if norm_w is None:
        return pl.pallas_call(
            _mm_res_body, grid=(m // tm, n // COL_TILE), in_specs=[a_spec, w_spec, o_spec], out_specs=o_spec,
            out_shape=jax.ShapeDtypeStruct((m, n), F32),
            compiler_params=_cparams(2), name=name,
        )(a, w, res)
    return pl.pallas_call(
        _mm_res_norm_body, grid=(m // tm, n // COL_TILE),
        in_specs=[a_spec, w_spec, o_spec, pl.BlockSpec((1, COL_TILE), lambda i, j: (0, j))],
        out_specs=[o_spec, o_spec, pl.BlockSpec((tm, LANE), lambda i, j: (i, 0))],
        out_shape=[jax.ShapeDtypeStruct((m, n), F32), jax.ShapeDtypeStruct((m, n), BF16),
                   jax.ShapeDtypeStruct((m, LANE), F32)],
        compiler_params=_cparams(2), name=name,
    )(a, w, res, norm_w.reshape(1, n))


def _row_spec(width, col_start, seg):
    row_base, C, B, nblk = seg
    rb, cb = row_base // C, col_start // width
    return pl.BlockSpec((C, width), lambda b, i: (rb + b * nblk + i, cb))


def _state_spec(shape):
    nd = len(shape)
    return pl.BlockSpec((1,) + tuple(shape[1:]), lambda b, i: (b,) + (0,) * (nd - 1))


def _param_spec(shape):
    nd = len(shape)
    return pl.BlockSpec(tuple(shape), lambda b, i: (0,) * nd)


def _seq_call(body, seg, row_ins, state_ins, params, buf, out_width, out_col, scratch, name, in_row_base=None):
    row_base, C, B, nblk = seg
    in_seg = seg if in_row_base is None else (in_row_base, C, B, nblk)
    in_specs = [_row_spec(w, c, in_seg) for (_, w, c) in row_ins]
    in_specs += [_state_spec(s.shape) for s in state_ins]
    in_specs += [_param_spec(p.shape) for p in params]
    in_specs += [pl.BlockSpec(memory_space=pl.ANY)]
    args = [a for (a, _, _) in row_ins] + list(state_ins) + list(params) + [buf]
    out_specs = [_row_spec(out_width, out_col, seg)] + [_state_spec(s.shape) for s in state_ins]
    out_shape = [jax.ShapeDtypeStruct(buf.shape, buf.dtype)] + [jax.ShapeDtypeStruct(s.shape, s.dtype) for s in state_ins]
    return pl.pallas_call(
        functools.partial(body, C, nblk), grid=(B, nblk), in_specs=in_specs, out_specs=out_specs,
        out_shape=out_shape, scratch_shapes=scratch,
        input_output_aliases={len(args) - 1: 0},
        compiler_params=_cparams(2), name=name,
    )(*args)


def _rwkv_main(C, p, carry, s_scr, prm):
    mu, w0, wup, a0, aup, gup, kkw, kaw, rkw, lnw, lnb = prm
    rows = lax.broadcasted_iota(jnp.int32, (C, 1), 0)
    prev = jnp.where(rows == 0, carry[...], pltpu.roll(p, 1, 0))
    carry[...] = p[C - 1:C, :]
    xs = p + (prev - p) * mu[...]
    r, k, v = xs[:, 0:RW_W], xs[:, RW_W:2 * RW_W], xs[:, 2 * RW_W:3 * RW_W]
    dw, da, dg = xs[:, 3 * RW_W:3 * RW_W + 64], xs[:, 3 * RW_W + 64:3 * RW_W + 128], xs[:, 3 * RW_W + 128:RW_P]
    lw = -math.exp(-0.5) * _sigmoid(w0[...] + _bdot(jnp.tanh(dw), wup[...]))
    a = _sigmoid(a0[...] + _bdot(da, aup[...]))
    g = _bdot(_sigmoid(dg), gup[...])
    sel = _head_selectors(RW_W, RW_HD)
    kkx = k * kkw[...]
    kkn = kkx * lax.rsqrt(_head_sums(kkx * kkx, sel) + 1e-6)
    kt = k * (1.0 + (a - 1.0) * kaw[...])
    bonus = _head_sums(r * kt * rkw[...], sel) * v
    kb = kkn * a
    yield

    tril, strict, eye = _tri_masks(C)
    cl = _cumsum_rows(lw, tril.astype(BF16))
    w_in = jnp.exp(cl)
    w_ex = jnp.exp(cl - lw)
    w_inv = jnp.exp(-cl)
    w_end = jnp.exp(cl[C - 1:C, :] - cl)

    heads = range(RW_HEADS)
    sls = [slice(RW_HD * h, RW_HD * (h + 1)) for h in heads]
    kk_ex, r_in = kkn * w_ex, r * w_in
    kt_inv, kb_inv = kt * w_inv, kb * w_inv
    kt_end, kb_end = kt * w_end, kb * w_end
    yield
    lhss = [jnp.concatenate([kk_ex[:, sl], r_in[:, sl]], axis=0) for sl in sls]
    rhss = [jnp.concatenate([kt_inv[:, sl], kb_inv[:, sl]], axis=0) for sl in sls]
    scs = [_bdot_nt(lhs, rhs) for lhs, rhs in zip(lhss, rhss)]
    yield
    s0s = [s_scr[h] for h in heads]
    pss = [_bdot_nt(lhs, s0) for lhs, s0 in zip(lhss, s0s)]
    yield
    negs = [-(ps[:C] + _bdot(jnp.where(strict, sc[:C, :C], 0.0), v[:, sl])) for ps, sc, sl in zip(pss, scs, sls)]
    yield
    invs = yield from _unit_lower_inverse([jnp.where(strict, sc[:C, C:], 0.0) for sc in scs], eye, C)
    us = [_bdot(t, n) for t, n in zip(invs, negs)]
    yield
    vus = [jnp.concatenate([v[:, sl], u], axis=0) for u, sl in zip(us, sls)]
    a_rs = [jnp.concatenate([jnp.where(tril, sc[C:, :C], 0.0), jnp.where(tril, sc[C:, C:], 0.0)], axis=1) for sc in scs]
    ys = [ps[C:] + _bdot(a_r, vu) for ps, a_r, vu in zip(pss, a_rs, vus)]
    yield
    kes = [jnp.concatenate([kt_end[:, sl], kb_end[:, sl]], axis=0) for sl in sls]
    for h in heads:
        s_scr[h] = s0s[h] * w_in[C - 1:C, sls[h]] + _bdot_tn(vus[h], kes[h])
    yield
    y = jnp.concatenate(ys, axis=1)
    yc = y - _head_sums(y, sel) * (1.0 / RW_HD)
    yn = yc * lax.rsqrt(_head_sums(yc * yc, sel) * (1.0 / RW_HD) + RW_GN_EPS)
    return (yn * lnw[...] + lnb[...] + bonus) * g


def _gdn_main(C, x, z, ab, tail, s_scr, prm):
    convw, alog, dtb, normw = prm
    t8 = tail[...]
    conv = x * convw[3:4, :]
    for s in (1, 2, 3):
        conv = conv + _shift_rows(x, t8, s, C) * convw[3 - s:4 - s, :]
    tail[...] = x[C - TAIL:, :]
    act = conv * _sigmoid(conv)
    q, k, v = act[:, 0:GD_W], act[:, GD_W:2 * GD_W], act[:, 2 * GD_W:3 * GD_W]
    yield

    g_all = -jnp.exp(alog[...]) * _softplus(ab + dtb[...])
    beta_all = _sigmoid(ab)
    tril, strict, eye = _tri_masks(C)
    gc = _cumsum_rows(g_all, tril.astype(BF16))
    gr = _cols_to_rows(gc, 16)
    eg = jnp.exp(gc)
    yield

    heads = range(GD_HEADS)
    sls = [slice(GD_HD * h, GD_HD * (h + 1)) for h in heads]
    qs = [q[:, sl] for sl in sls]
    qs = [t * (lax.rsqrt(jnp.sum(t * t, -1, keepdims=True) + 1e-6) * GD_HD ** -0.5) for t in qs]
    ks = [k[:, sl] for sl in sls]
    ks = [t * lax.rsqrt(jnp.sum(t * t, -1, keepdims=True) + 1e-6) for t in ks]
    yield
    betas = [beta_all[:, GD_HEADS + h:GD_HEADS + h + 1] for h in heads]
    decays = [jnp.where(tril, jnp.exp(jnp.where(tril, gc[:, h:h + 1] - gr[h:h + 1, :], 0.0)), 0.0) for h in heads]
    yield
    scs = [_bdot_nt(jnp.concatenate([k_h, q_h], axis=0), k_h) for k_h, q_h in zip(ks, qs)]
    yield
    invs = yield from _unit_lower_inverse(
        [jnp.where(strict, beta * sc[:C] * decay, 0.0) for beta, sc, decay in zip(betas, scs, decays)], eye, C)
    rhss = [jnp.concatenate([v[:, sl] * beta, k_h * (beta * eg[:, h:h + 1])], axis=1)
            for h, sl, beta, k_h in zip(heads, sls, betas, ks)]
    sols = [_bdot(t, rhs) for t, rhs in zip(invs, rhss)]
    yield
    s0s = [s_scr[h] for h in heads]
    pss = [_bdot_nt(jnp.concatenate([sol[:, GD_HD:], q_h], axis=0), s0) for sol, q_h, s0 in zip(sols, qs, s0s)]
    yield
    us = [sol[:, :GD_HD] - ps[:C] for sol, ps in zip(sols, pss)]
    os_ = [eg[:, h:h + 1] * ps[C:] + _bdot(sc[C:] * decay, u) for h, ps, sc, decay, u in zip(heads, pss, scs, decays, us)]
    yield
    for h in heads:
        gl = gc[C - 1:C, h:h + 1]
        s_scr[h] = jnp.exp(gl) * s0s[h] + _bdot_tn(us[h] * jnp.exp(gl - gc[:, h:h + 1]), ks[h])
    yield
    outs = [o * lax.rsqrt(jnp.mean(o * o, -1, keepdims=True) + NORM_EPS) * normw[...] for o in os_]
    return jnp.concatenate(outs, axis=1) * (z * _sigmoid(z))


def _mlstm_main(C, q, k, v, og, gates, c_scr, n_scr, m_scr, prm):
    ib, fb, normw = prm
    li_all = gates + ib[...]
    x = gates + fb[...]
    lf_all = jnp.minimum(x, 0.0) - jnp.log(1.0 + jnp.exp(-jnp.abs(x)))
    tril, _, _ = _tri_masks(C)
    fc = _cumsum_rows(lf_all, tril.astype(BF16))
    fr = _cols_to_rows(fc, 8)
    lir = _cols_to_rows(li_all, 8)
    lane = lax.broadcasted_iota(jnp.int32, (1, LANE), 1)
    m_all = m_scr[...]
    m_new = m_all
    yield

    heads = range(ML_HEADS)
    qs = [q[:, ML_DK * h:ML_DK * (h + 1)] * ML_DK ** -0.5 for h in heads]
    ks = [k[:, ML_DK * h:ML_DK * (h + 1)] for h in heads]
    vs = [v[:, ML_DV * h:ML_DV * (h + 1)] for h in heads]
    c0s = [c_scr[h] for h in heads]
    n0s = [n_scr[h:h + 1, :] for h in heads]
    qks = [_bdot_nt(q_h, k_h) for q_h, k_h in zip(qs, ks)]
    qcs = [_bdot_nt(q_h, c0) for q_h, c0 in zip(qs, c0s)]
    yield
    fcs = [fc[:, ML_HEADS + h:ML_HEADS + h + 1] for h in heads]
    m0s = [m_all[:, h:h + 1] for h in heads]
    dmats = [jnp.where(tril, fc_h - fr[ML_HEADS + h:ML_HEADS + h + 1, :] + lir[h:h + 1, :], -jnp.inf)
             for h, fc_h in zip(heads, fcs)]
    inters = [fc_h + m0 for fc_h, m0 in zip(fcs, m0s)]
    ms = [jnp.maximum(inter, jnp.max(dmat, -1, keepdims=True)) for inter, dmat in zip(inters, dmats)]
    yield
    ss = [qk * jnp.exp(dmat - m) for qk, dmat, m in zip(qks, dmats, ms)]
    wis = [jnp.exp(inter - m) for inter, m in zip(inters, ms)]
    yield
    nums = [wi * qc + _bdot(s, v_h) for wi, qc, s, v_h in zip(wis, qcs, ss, vs)]
    dens = [wi * jnp.sum(q_h * n0, -1, keepdims=True) + jnp.sum(s, -1, keepdims=True)
            for wi, q_h, n0, s in zip(wis, qs, n0s, ss)]
    yield
    hhs = [num / jnp.maximum(jnp.abs(den), jnp.exp(-m)) for num, den, m in zip(nums, dens, ms)]
    m_cs = [m[C - 1:C, :] for m in ms]
    wends = [jnp.exp(fc_h[C - 1:C, :] - fc_h + li_all[:, h:h + 1] - m_c) for h, fc_h, m_c in zip(heads, fcs, m_cs)]
    dstates = [jnp.exp(fc_h[C - 1:C, :] + m0 - m_c) for fc_h, m0, m_c in zip(fcs, m0s, m_cs)]
    for h in heads:
        c_scr[h] = dstates[h] * c0s[h] + _bdot_tn(vs[h] * wends[h], ks[h])
        n_scr[h:h + 1, :] = dstates[h] * n0s[h] + jnp.sum(wends[h] * ks[h], 0, keepdims=True)
        m_new = jnp.where(lane == h, m_cs[h], m_new)
    m_scr[...] = m_new
    yield
    outs = [hh * lax.rsqrt(jnp.mean(hh * hh, -1, keepdims=True) + NORM_EPS) for hh in hhs]
    return jnp.concatenate(outs, axis=1) * normw[...] * _sigmoid(og)


N_MIX_STATES = 7
N_RW_PRM, N_GD_PRM, N_ML_PRM = 11, 4, 3


def _mixer_body(C, nblk, p_ref, *refs):
    ins, refs = refs[:N_MIX_STATES], refs[N_MIX_STATES:]
    rw_prm, refs = refs[:N_RW_PRM], refs[N_RW_PRM:]
    gd_prm, refs = refs[:N_GD_PRM], refs[N_GD_PRM:]
    ml_prm, refs = refs[:N_ML_PRM], refs[N_ML_PRM:]
    o_ref, refs = refs[1], refs[2:]
    outs, scr = refs[:N_MIX_STATES], refs[N_MIX_STATES:]
    carry, s_rw, tail, s_gd, c_scr, n_scr, m_scr = scr
    i = pl.program_id(1)

    @pl.when(i == 0)
    def _():
        for dst, src in zip(scr, ins):
            dst[...] = src[0]

    pm, pt = p_ref[:, 0:P_MAIN], p_ref[:, P_MAIN:P_MAIN + P_TAIL]
    o_rw, o_gd, o_ml = _interleave([
        _rwkv_main(C, pm[:, 0:RW_P], carry, s_rw, rw_prm),
        _gdn_main(C, pm[:, RW_P:P_MAIN], pt[:, T_Z:T_Z + GD_W], pt[:, T_AB:T_AB + LANE], tail, s_gd, gd_prm),
        _mlstm_main(C, pt[:, T_MLQ:T_MLQ + ML_QK], pt[:, T_MLK:T_MLK + ML_QK], pt[:, T_MLV:T_MLV + ML_W],
                    pt[:, T_MLO:T_MLO + ML_W], pt[:, T_IF:T_IF + LANE], c_scr, n_scr, m_scr, ml_prm)])
    o_ref[:, 0:RW_W] = o_rw.astype(o_ref.dtype)
    o_ref[:, RW_W:RW_W + GD_W] = o_gd.astype(o_ref.dtype)
    o_ref[:, RW_W + GD_W:] = o_ml.astype(o_ref.dtype)

    @pl.when(i == nblk - 1)
    def _():
        outs[0][0] = pm[C - 1:C, 0:RW_P]
        for dst, src in zip(outs[1:], scr[1:]):
            dst[0] = src[...]


def _mixer_call(p, buf, seg, states, params):
    scratch = [pltpu.VMEM((1, RW_P), F32), pltpu.VMEM((RW_HEADS, RW_HD, RW_HD), F32),
               pltpu.VMEM((TAIL, 3 * GD_W), F32), pltpu.VMEM((GD_HEADS, GD_HD, GD_HD), F32),
               pltpu.VMEM((ML_HEADS, ML_DV, ML_DK), F32), pltpu.VMEM((ML_HEADS, ML_DK), F32), pltpu.VMEM((1, LANE), F32)]
    outs = _seq_call(_mixer_body, seg, [(p, P_MAIN + P_TAIL, 0)], list(states), params,
                     buf, D_MODEL, 0, scratch, "mixers")
    return outs[0], list(outs[1:])


def _ffn_act_body(C, nblk, g_ref, u_ref, tail_in, convw, convb, _buf, o_ref, tail_out, tail):
    i = pl.program_id(1)

    @pl.when(i == 0)
    def _():
        tail[...] = tail_in[0]

    x = g_ref[...]
    t8 = tail[...]
    conv = x * convw[2:3, :] + convb[...]
    for s in (1, 2):
        conv = conv + _shift_rows(x, t8, s, C) * convw[2 - s:3 - s, :]
    tail[...] = x[C - TAIL:, :]
    o_ref[...] = (conv * _sigmoid(conv) * u_ref[...]).astype(o_ref.dtype)

    @pl.when(i == nblk - 1)
    def _():
        tail_out[0] = tail[...]


def _ffn_act_call(g, u, buf, seg, tail, params, in_row_base=None):
    outs = _seq_call(
        _ffn_act_body, seg, [(g, D_FF, 0), (u, D_FF, 0)], [tail], params, buf, D_FF, 0,
        [pltpu.VMEM((TAIL, D_FF), F32)], "ffn_act", in_row_base)
    return outs[0], outs[1]


def _ffn_up_body(tm, h_ref, ssq_ref, wg_ref, wu_ref, convw, convb, o_ref, gt_ref, ut_ref, carry):
    i, j = pl.program_id(0), pl.program_id(1)
    h = h_ref[...]
    scale = _row_scale(ssq_ref)
    g = scale * jnp.dot(h, wg_ref[...].astype(BF16), preferred_element_type=F32)
    u = scale * jnp.dot(h, wu_ref[...].astype(BF16), preferred_element_type=F32)
    gt_ref[...] = g[tm - FFN_TAIL_ROWS:, :]
    ut_ref[...] = u[tm - FFN_TAIL_ROWS:, :]
    rows = i * tm + lax.broadcasted_iota(jnp.int32, (tm, 1), 0)
    g = jnp.where(rows >= ROW_META, g, 0.0)

    @pl.when(i == 0)
    def _():
        carry[j] = jnp.zeros((TAIL, COL_TILE), F32)

    t8 = carry[j]
    conv = g * convw[2:3, :] + convb[...]
    for s in (1, 2):
        conv = conv + _shift_rows(g, t8, s, tm) * convw[2 - s:3 - s, :]
    carry[j] = g[tm - TAIL:, :]
    o_ref[...] = (conv * _sigmoid(conv) * u).astype(o_ref.dtype)


def _ffn_up(hb, ssq, w_gate, w_up, layer, convw, convb, tm):
    m, k = hb.shape
    n = w_gate.shape[2]
    w_spec = pl.BlockSpec((None, k, COL_TILE), lambda i, j: (layer, 0, j))
    t_spec = pl.BlockSpec((FFN_TAIL_ROWS, COL_TILE), lambda i, j: (i, j))
    return pl.pallas_call(
        functools.partial(_ffn_up_body, tm), grid=(m // tm, n // COL_TILE),
        in_specs=[pl.BlockSpec((tm, k), lambda i, j: (i, 0), pipeline_mode=pl.Buffered(1)),
                  pl.BlockSpec((tm, LANE), lambda i, j: (i, 0)), w_spec, w_spec,
                  pl.BlockSpec((3, COL_TILE), lambda i, j: (0, j)), pl.BlockSpec((1, COL_TILE), lambda i, j: (0, j))],
        out_specs=[pl.BlockSpec((tm, COL_TILE), lambda i, j: (i, j)), t_spec, t_spec],
        out_shape=[jax.ShapeDtypeStruct((m, n), BF16), jax.ShapeDtypeStruct((m // tm * FFN_TAIL_ROWS, n), F32),
                   jax.ShapeDtypeStruct((m // tm * FFN_TAIL_ROWS, n), F32)],
        scratch_shapes=[pltpu.VMEM((n // COL_TILE, TAIL, COL_TILE), F32)],
        compiler_params=_cparams(2), name="ffn_up_act",
    )(hb, ssq, w_gate, w_up, convw, convb)


def _pad_cols(x, width):
    return jnp.pad(x, ((0, 0),) * (x.ndim - 1) + ((0, width - x.shape[-1]),))


def _lane_vec(x, start=0):
    return jnp.zeros((1, LANE), F32).at[0, start:start + x.shape[0]].set(x)


def _tail_cols(w):
    ml0 = RW_P + GD_P
    gd_ab = w[..., P_MAIN:P_MAIN + 2 * GD_HEADS]
    ml_if = w[..., ml0 + 2 * ML_QK + ML_W:ml0 + 2 * ML_QK + ML_W + 2 * ML_HEADS]
    cols = [_pad_cols(gd_ab, LANE), _pad_cols(ml_if, LANE), w[..., ml0:ml0 + 2 * ML_QK],
            w[..., P_MAIN + 2 * GD_HEADS:ml0], w[..., ml0 + 2 * ML_QK:ml0 + 2 * ML_QK + ML_W],
            w[..., ml0 + 2 * ML_QK + ML_W + 2 * ML_HEADS:ml0 + ML_P]]
    return jnp.concatenate(cols, axis=-1)


def _pad_tail(buf, width):
    b, r, w = buf.shape
    return jnp.pad(buf, ((0, 0), (TAIL - r, 0), (0, width - w)))


def _layer(x, xw, ssq, states, lp, wts, layer, segs):
    n = x.shape[0]
    p = _matmul_wide(xw, ssq, wts["w_in"], layer, name="proj_in")
    mix = jnp.zeros((n, D_MODEL), BF16)
    new_states = []
    for seg, (mix_st, ffn_tail) in zip(segs, states):
        if mix_st is None:
            mix_st = new_states[-1][0]
        mix, mix_new = _mixer_call(p, mix, seg, mix_st, lp["mixers"])
        new_states.append([mix_new, ffn_tail])
    x, xw, ssq = _matmul(mix, wts["w_out"], layer, ROW_TILE, x, norm_w=lp["norm_ffn_w"], name="proj_out")
    act, g_tail, u_tail = _ffn_up(xw, ssq, wts["w_gate"], wts["w_up"], layer, lp["ffn"][0], lp["ffn"][1], ROW_TILE)
    tail_base = n - g_tail.shape[0]
    for idx, seg in enumerate(segs):
        if idx == 0 or states[idx][0] is None:
            end = seg[0] + seg[1] * seg[2] * seg[3]
            kept = end - TAIL >= n - FFN_TAIL_ROWS
            new_states[idx][1] = g_tail[end - TAIL - tail_base:end - tail_base][None] if kept else None
        else:
            assert seg[0] >= n - FFN_TAIL_ROWS
            act, new_states[idx][1] = _ffn_act_call(g_tail, u_tail, act, seg, new_states[idx][1], lp["ffn"],
                                                    in_row_base=seg[0] - tail_base)
    if lp["next_norm_w"] is None:
        return _matmul(act, wts["w_down"], layer, ROW_TILE_DOWN, x, name="ffn_down"), None, None, new_states
    x, xw, ssq = _matmul(act, wts["w_down"], layer, ROW_TILE_DOWN, x, norm_w=lp["next_norm_w"], name="ffn_down")
    return x, xw, ssq, new_states


def kernel(x_prompt, x_sample, state_rwkv_wkv, state_rwkv_shift, state_gdn, cache_gdn_conv, state_mlstm_c, state_mlstm_n, state_mlstm_m, cache_ffn_conv, meta_tokens, norm_mix_w, w_in, rwkv_mu, rwkv_w0, rwkv_w_up, rwkv_a0, rwkv_a_up, rwkv_g_up, rwkv_k_k, rwkv_k_a, rwkv_r_k, rwkv_ln_w, rwkv_ln_b, gdn_conv_w, gdn_a_log, gdn_dt_bias, gdn_norm_w, mlstm_i_b, mlstm_f_b, mlstm_norm_w, w_out, norm_ffn_w, ffn_w_gate, ffn_w_up, ffn_conv_w, ffn_conv_b, ffn_w_down, final_norm_w):
    depth = w_in.shape[0]
    seq = x_prompt.shape[1]
    dec_b, dec_seq = x_sample.shape[0], x_sample.shape[1]
    row_sample = ROW_PROMPT + seq
    n_rows = row_sample + dec_b * dec_seq
    n_pad = -(-n_rows // ROW_TILE) * ROW_TILE
    segs = ((ROW_META, N_META, 1, 1), (ROW_PROMPT, CHUNK, 1, seq // CHUNK), (row_sample, dec_seq, dec_b, 1))

    x = jnp.concatenate([
        jnp.zeros((ROW_META, D_MODEL), F32), meta_tokens.astype(F32), x_prompt[0],
        x_sample.reshape(dec_b * dec_seq, D_MODEL), jnp.zeros((n_pad - n_rows, D_MODEL), F32)], axis=0)

    row = lambda a: a.reshape(1, -1)
    zero_mix = [jnp.zeros((1, 1, RW_P), F32), jnp.zeros((1, RW_HEADS, RW_HD, RW_HD), F32),
                jnp.zeros((1, TAIL, 3 * GD_W), F32), jnp.zeros((1, GD_HEADS, GD_HD, GD_HD), F32),
                jnp.zeros((1, ML_HEADS, ML_DV, ML_DK), F32), jnp.zeros((1, ML_HEADS, ML_DK), F32),
                jnp.zeros((1, 1, LANE), F32)]

    wts = {
        "w_in": jnp.concatenate([w_in[..., :P_MAIN], _tail_cols(w_in)], axis=-1).astype(BF16),
        "w_out": w_out,
        "w_gate": ffn_w_gate,
        "w_up": ffn_w_up,
        "w_down": ffn_w_down.astype(BF16),
    }

    xw, ssq = _prenorm(x, norm_mix_w[0], ROW_TILE_SMALL)
    p_out, s_out = [], []
    for l in range(depth):
        lp = {
            "next_norm_w": norm_mix_w[l + 1] if l + 1 < depth else None,
            "mixers": [row(rwkv_mu[l]), row(rwkv_w0[l]), rwkv_w_up[l], row(rwkv_a0[l]), rwkv_a_up[l], rwkv_g_up[l],
                       row(rwkv_k_k[l]), row(rwkv_k_a[l]), row(rwkv_r_k[l]), row(rwkv_ln_w[l]), row(rwkv_ln_b[l]),
                       gdn_conv_w[l], _lane_vec(gdn_a_log[l]), _lane_vec(gdn_dt_bias[l]), row(gdn_norm_w[l]),
                       _lane_vec(mlstm_i_b[l]), _lane_vec(mlstm_f_b[l], ML_HEADS), row(mlstm_norm_w[l])],
            "norm_ffn_w": norm_ffn_w[l],
            "ffn": [ffn_conv_w[l], row(ffn_conv_b[l])],
        }
        sample_mix = [state_rwkv_shift[l][:, None, :], state_rwkv_wkv[l],
                      _pad_tail(cache_gdn_conv[l], 3 * GD_W), state_gdn[l],
                      state_mlstm_c[l], state_mlstm_n[l], _pad_cols(state_mlstm_m[l], LANE)[:, None, :]]
        states = ([zero_mix, None], [None, None], [sample_mix, _pad_tail(cache_ffn_conv[l], D_FF)])
        x, xw, ssq, st = _layer(x, xw, ssq, states, lp, wts, l, segs)
        p_out.append(st[1])
        s_out.append(st[2])

    y_prompt = _rmsnorm(x, final_norm_w, F32, seq, ROW_PROMPT, CHUNK).reshape(1, seq, D_MODEL)
    y_sample = _rmsnorm(x, final_norm_w, F32, dec_b * dec_seq, row_sample, CHUNK).reshape(dec_b, dec_seq, D_MODEL)

    def collect(sts):
        stack = lambda f: jnp.stack([f(st) for st in sts], 0)
        return (stack(lambda st: st[0][1]), stack(lambda st: st[0][0][:, 0, :]),
                stack(lambda st: st[0][3]), stack(lambda st: st[0][2][:, TAIL - 3:, :]),
                stack(lambda st: st[0][4]), stack(lambda st: st[0][5]), stack(lambda st: st[0][6][:, 0, :ML_HEADS]),
                stack(lambda st: st[1][:, TAIL - 2:, :]))

    return (y_prompt, y_sample) + collect(p_out) + collect(s_out)
```

```python
import functools
import math

import jax
import jax.numpy as jnp
from jax import lax
from jax.experimental import pallas as pl
from jax.experimental.pallas import tpu as pltpu

F32 = jnp.float32
BF16 = jnp.bfloat16

D_MODEL = 4096
N_META = 16
CHUNK = 64
NORM_EPS = 1e-6
RW_HEADS, RW_HD = 24, 64
RW_W = RW_HEADS * RW_HD
RW_P = 3 * RW_W + 64 + 64 + 128
RW_GN_EPS = 64e-5
GD_HEADS, GD_HD = 12, 128
GD_W = GD_HEADS * GD_HD
GD_P = 4 * GD_W + 2 * GD_HEADS
ML_HEADS, ML_DK, ML_DV = 4, 128, 256
ML_W = ML_HEADS * ML_DV
ML_QK = ML_HEADS * ML_DK
ML_P = 2 * ML_QK + 2 * ML_W + 2 * ML_HEADS
D_FF = 11008

LANE = 128
TAIL = 8
ROW_META = 48
ROW_PROMPT = 64
ROW_TILE = 2128
ROW_TILE_DOWN = 1216
ROW_TILE_SMALL = 448
COL_TILE = 256
FFN_TAIL_ROWS = 512
VMEM_LIMIT = 60 * 1024 * 1024

P_MAIN = RW_P + 3 * GD_W
T_AB = 0
T_IF = 128
T_MLQ = 256
T_MLK = 768
T_Z = 1280
T_MLV = 2816
T_MLO = 3840
P_TAIL = 4864


def _cparams(n_axes):
    return pltpu.CompilerParams(dimension_semantics=("arbitrary",) * n_axes, vmem_limit_bytes=VMEM_LIMIT)


def _bdot(a, b):
    return jnp.dot(a.astype(BF16), b.astype(BF16), preferred_element_type=F32)


def _bdot_nt(a, b):
    return lax.dot_general(a.astype(BF16), b.astype(BF16), (((1,), (1,)), ((), ())), preferred_element_type=F32)


def _bdot_tn(a, b):
    return lax.dot_general(a.astype(BF16), b.astype(BF16), (((0,), (0,)), ((), ())), preferred_element_type=F32)


def _split3(x):
    hi = x.astype(BF16)
    r1 = x - hi.astype(F32)
    mid = r1.astype(BF16)
    lo = (r1 - mid.astype(F32)).astype(BF16)
    return hi, mid, lo


def _cumsum_rows(x, tril_bf):
    hi, mid, lo = _split3(x)
    d = lambda t: jnp.dot(tril_bf, t, preferred_element_type=F32)
    return d(hi) + d(mid) + d(lo)


def _cols_to_rows(x, n_rows):
    sel = (lax.broadcasted_iota(jnp.int32, (n_rows, LANE), 0) == lax.broadcasted_iota(jnp.int32, (n_rows, LANE), 1)).astype(BF16)
    hi, mid, lo = _split3(x)
    d = lambda t: lax.dot_general(sel, t, (((1,), (1,)), ((), ())), preferred_element_type=F32)
    return d(hi) + d(mid) + d(lo)


def _head_selectors(width, hd):
    sh = int(math.log2(hd))
    e = (lax.broadcasted_iota(jnp.int32, (width, LANE), 0) >> sh) == lax.broadcasted_iota(jnp.int32, (width, LANE), 1)
    et = lax.broadcasted_iota(jnp.int32, (LANE, width), 0) == (lax.broadcasted_iota(jnp.int32, (LANE, width), 1) >> sh)
    return e.astype(BF16), et.astype(BF16)


def _head_sums(x, sel):
    e, et = sel
    hi, lo, _ = _split3(x)
    d = lambda t: jnp.dot(t, e, preferred_element_type=F32)
    hi, lo, _ = _split3(d(hi) + d(lo))
    d = lambda t: jnp.dot(t, et, preferred_element_type=F32)
    return d(hi) + d(lo)


def _tri_masks(C):
    ri = lax.broadcasted_iota(jnp.int32, (C, C), 0)
    ci = lax.broadcasted_iota(jnp.int32, (C, C), 1)
    return ri >= ci, ri > ci, ri == ci


def _unit_lower_inverse(As, eye, C):
    Ps = [-A for A in As]
    Ts = [jnp.where(eye, 1.0, P) for P in Ps]
    levels = int(math.log2(C))
    if levels > 1:
        Ps = [_bdot(P, P) for P in Ps]
        yield
    for k in range(1, levels):
        if k < levels - 1:
            Ys = [_bdot(jnp.concatenate([T, P], axis=0), P) for T, P in zip(Ts, Ps)]
            Ts = [T + Y[:C] for T, Y in zip(Ts, Ys)]
            Ps = [Y[C:] for Y in Ys]
        else:
            Ts = [T + _bdot(T, P) for T, P in zip(Ts, Ps)]
        yield
    return Ts


def _interleave(gens):
    results, alive = [None] * len(gens), list(range(len(gens)))
    while alive:
        for idx in list(alive):
            try:
                next(gens[idx])
            except StopIteration as stop:
                results[idx] = stop.value
                alive.remove(idx)
    return results


def _sigmoid(x):
    return jax.nn.sigmoid(x)


def _softplus(x):
    return jnp.maximum(x, 0.0) + jnp.log(1.0 + jnp.exp(-jnp.abs(x)))


def _shift_rows(x, tail, s, C):
    xr = pltpu.roll(x, s, 0)
    r8 = lax.broadcasted_iota(jnp.int32, (TAIL, 1), 0)
    top = jnp.where(r8 < s, pltpu.roll(tail, s, 0), xr[0:TAIL])
    return jnp.concatenate([top, xr[TAIL:]], axis=0)


def _rmsnorm_body(x_ref, w_ref, o_ref):
    x = x_ref[...]
    y = x * lax.rsqrt(jnp.mean(x * x, -1, keepdims=True) + NORM_EPS)
    o_ref[...] = (y * w_ref[...]).astype(o_ref.dtype)


def _rmsnorm(x, w, out_dtype, rows, row_base, block):
    d = x.shape[1]
    rb = row_base // block
    return pl.pallas_call(
        _rmsnorm_body,
        grid=(rows // block,),
        in_specs=[pl.BlockSpec((block, d), lambda i: (rb + i, 0)), pl.BlockSpec((1, d), lambda i: (0, 0))],
        out_specs=pl.BlockSpec((block, d), lambda i: (i, 0)),
        out_shape=jax.ShapeDtypeStruct((rows, d), out_dtype),
        compiler_params=_cparams(1),
        name="rmsnorm",
    )(x, w.reshape(1, d))


def _lane_fold(x):
    return functools.reduce(lambda a, b: a + b, [x[:, c:c + LANE] for c in range(0, x.shape[1], LANE)])


def _row_scale(ssq_ref):
    return lax.rsqrt(jnp.sum(ssq_ref[...], -1, keepdims=True) * (1.0 / D_MODEL) + NORM_EPS)


def _prenorm_body(x_ref, w_ref, xw_ref, ssq_ref):
    x = x_ref[...]
    xw_ref[...] = (x * w_ref[...]).astype(xw_ref.dtype)
    ssq_ref[...] = _lane_fold(x * x)


def _prenorm(x, w, block):
    n, d = x.shape
    return pl.pallas_call(
        _prenorm_body, grid=(n // block,),
        in_specs=[pl.BlockSpec((block, d), lambda i: (i, 0)), pl.BlockSpec((1, d), lambda i: (0, 0))],
        out_specs=[pl.BlockSpec((block, d), lambda i: (i, 0)), pl.BlockSpec((block, LANE), lambda i: (i, 0))],
        out_shape=[jax.ShapeDtypeStruct((n, d), BF16), jax.ShapeDtypeStruct((n, LANE), F32)],
        compiler_params=_cparams(1), name="prenorm",
    )(x, w.reshape(1, d))


def _mm_scaled_body(a_ref, ssq_ref, w_ref, o_ref):
    o_ref[...] = _row_scale(ssq_ref) * jnp.dot(a_ref[...], w_ref[...].astype(BF16), preferred_element_type=F32)


def _mm_res_norm_body(a_ref, w_ref, r_ref, nw_ref, o_ref, xw_ref, ssq_ref):
    j = pl.program_id(1)
    acc = r_ref[...] + jnp.dot(a_ref[...], w_ref[...].astype(BF16), preferred_element_type=F32)
    o_ref[...] = acc
    xw_ref[...] = (acc * nw_ref[...]).astype(xw_ref.dtype)
    part = _lane_fold(acc * acc)

    @pl.when(j == 0)
    def _():
        ssq_ref[...] = part

    @pl.when(j > 0)
    def _():
        ssq_ref[...] += part


def _mm_res_body(a_ref, w_ref, r_ref, o_ref):
    o_ref[...] = r_ref[...] + jnp.dot(a_ref[...], w_ref[...].astype(BF16), preferred_element_type=F32)


def _matmul_wide(a, ssq, w, layer, name="matmul_wide"):
    m, k = a.shape
    n = w.shape[2]
    tm, tn = ROW_TILE_DOWN, 4 * COL_TILE
    return pl.pallas_call(
        _mm_scaled_body, grid=(n // tn, m // tm),
        in_specs=[pl.BlockSpec((tm, k), lambda j, i: (i, 0)), pl.BlockSpec((tm, LANE), lambda j, i: (i, 0)),
                  pl.BlockSpec((None, k, tn), lambda j, i: (layer, 0, j))],
        out_specs=pl.BlockSpec((tm, tn), lambda j, i: (i, j)),
        out_shape=jax.ShapeDtypeStruct((m, n), F32),
        compiler_params=_cparams(2), name=name,
    )(a, ssq, w)


def _matmul(a, w, layer, tm, res, norm_w=None, name="matmul"):
    m, k = a.shape
    n = w.shape[2]
    a_spec = pl.BlockSpec((tm, k), lambda i, j: (i, 0), pipeline_mode=pl.Buffered(1))
    w_spec = pl.BlockSpec((None, k, COL_TILE), lambda i, j: (layer, 0, j))
    o_spec = pl.BlockSpec((tm, COL_TILE), lambda i, j: (i, j))
    if norm_w is None:
        return pl.pallas_call(
            _mm_res_body, grid=(m // tm, n // COL_TILE), in_specs=[a_spec, w_spec, o_spec], out_specs=o_spec,
            out_shape=jax.ShapeDtypeStruct((m, n), F32),
            compiler_params=_cparams(2), name=name,
        )(a, w, res)
    return pl.pallas_call(
        _mm_res_norm_body, grid=(m // tm, n // COL_TILE),
        in_specs=[a_spec, w_spec, o_spec, pl.BlockSpec((1, COL_TILE), lambda i, j: (0, j))],
        out_specs=[o_spec, o_spec, pl.BlockSpec((tm, LANE), lambda i, j: (i, 0))],
        out_shape=[jax.ShapeDtypeStruct((m, n), F32), jax.ShapeDtypeStruct((m, n), BF16),
                   jax.ShapeDtypeStruct((m, LANE), F32)],
        compiler_params=_cparams(2), name=name,
    )(a, w, res, norm_w.reshape(1, n))


def _row_spec(width, col_start, seg):
    row_base, C, B, nblk = seg
    rb, cb = row_base // C, col_start // width
    return pl.BlockSpec((C, width), lambda b, i: (rb + b * nblk + i, cb))


def _state_spec(shape):
    nd = len(shape)
    return pl.BlockSpec((1,) + tuple(shape[1:]), lambda b, i: (b,) + (0,) * (nd - 1))


def _param_spec(shape):
    nd = len(shape)
    return pl.BlockSpec(tuple(shape), lambda b, i: (0,) * nd)


def _seq_call(body, seg, row_ins, state_ins, params, buf, out_width, out_col, scratch, name, in_row_base=None):
    row_base, C, B, nblk = seg
    in_seg = seg if in_row_base is None else (in_row_base, C, B, nblk)
    in_specs = [_row_spec(w, c, in_seg) for (_, w, c) in row_ins]
    in_specs += [_state_spec(s.shape) for s in state_ins]
    in_specs += [_param_spec(p.shape) for p in params]
    in_specs += [pl.BlockSpec(memory_space=pl.ANY)]
    args = [a for (a, _, _) in row_ins] + list(state_ins) + list(params) + [buf]
    out_specs = [_row_spec(out_width, out_col, seg)] + [_state_spec(s.shape) for s in state_ins]
    out_shape = [jax.ShapeDtypeStruct(buf.shape, buf.dtype)] + [jax.ShapeDtypeStruct(s.shape, s.dtype) for s in state_ins]
    return pl.pallas_call(
        functools.partial(body, C, nblk), grid=(B, nblk), in_specs=in_specs, out_specs=out_specs,
        out_shape=out_shape, scratch_shapes=scratch,
        input_output_aliases={len(args) - 1: 0},
        compiler_params=_cparams(2), name=name,
    )(*args)


def _rwkv_main(C, p, carry, s_scr, prm):
    mu, w0, wup, a0, aup, gup, kkw, kaw, rkw, lnw, lnb = prm
    rows = lax.broadcasted_iota(jnp.int32, (C, 1), 0)
    prev = jnp.where(rows == 0, carry[...], pltpu.roll(p, 1, 0))
    carry[...] = p[C - 1:C, :]
    xs = p + (prev - p) * mu[...]
    r, k, v = xs[:, 0:RW_W], xs[:, RW_W:2 * RW_W], xs[:, 2 * RW_W:3 * RW_W]
    dw, da, dg = xs[:, 3 * RW_W:3 * RW_W + 64], xs[:, 3 * RW_W + 64:3 * RW_W + 128], xs[:, 3 * RW_W + 128:RW_P]
    lw = -math.exp(-0.5) * _sigmoid(w0[...] + _bdot(jnp.tanh(dw), wup[...]))
    a = _sigmoid(a0[...] + _bdot(da, aup[...]))
    g = _bdot(_sigmoid(dg), gup[...])
    sel = _head_selectors(RW_W, RW_HD)
    kkx = k * kkw[...]
    kkn = kkx * lax.rsqrt(_head_sums(kkx * kkx, sel) + 1e-6)
    kt = k * (1.0 + (a - 1.0) * kaw[...])
    bonus = _head_sums(r * kt * rkw[...], sel) * v
    kb = kkn * a
    yield

    tril, strict, eye = _tri_masks(C)
    cl = _cumsum_rows(lw, tril.astype(BF16))
    w_in = jnp.exp(cl)
    w_ex = jnp.exp(cl - lw)
    w_inv = jnp.exp(-cl)
    w_end = jnp.exp(cl[C - 1:C, :] - cl)

    heads = range(RW_HEADS)
    sls = [slice(RW_HD * h, RW_HD * (h + 1)) for h in heads]
    kk_ex, r_in = kkn * w_ex, r * w_in
    kt_inv, kb_inv = kt * w_inv, kb * w_inv
    kt_end, kb_end = kt * w_end, kb * w_end
    yield
    lhss = [jnp.concatenate([kk_ex[:, sl], r_in[:, sl]], axis=0) for sl in sls]
    rhss = [jnp.concatenate([kt_inv[:, sl], kb_inv[:, sl]], axis=0) for sl in sls]
    scs = [_bdot_nt(lhs, rhs) for lhs, rhs in zip(lhss, rhss)]
    yield
    s0s = [s_scr[h] for h in heads]
    pss = [_bdot_nt(lhs, s0) for lhs, s0 in zip(lhss, s0s)]
    yield
    negs = [-(ps[:C] + _bdot(jnp.where(strict, sc[:C, :C], 0.0), v[:, sl])) for ps, sc, sl in zip(pss, scs, sls)]
    yield
    invs = yield from _unit_lower_inverse([jnp.where(strict, sc[:C, C:], 0.0) for sc in scs], eye, C)
    us = [_bdot(t, n) for t, n in zip(invs, negs)]
    yield
    vus = [jnp.concatenate([v[:, sl], u], axis=0) for u, sl in zip(us, sls)]
    a_rs = [jnp.concatenate([jnp.where(tril, sc[C:, :C], 0.0), jnp.where(tril, sc[C:, C:], 0.0)], axis=1) for sc in scs]
    ys = [ps[C:] + _bdot(a_r, vu) for ps, a_r, vu in zip(pss, a_rs, vus)]
    yield
    kes = [jnp.concatenate([kt_end[:, sl], kb_end[:, sl]], axis=0) for sl in sls]
    for h in heads:
        s_scr[h] = s0s[h] * w_in[C - 1:C, sls[h]] + _bdot_tn(vus[h], kes[h])
    yield
    y = jnp.concatenate(ys, axis=1)
    yc = y - _head_sums(y, sel) * (1.0 / RW_HD)
    yn = yc * lax.rsqrt(_head_sums(yc * yc, sel) * (1.0 / RW_HD) + RW_GN_EPS)
    return (yn * lnw[...] + lnb[...] + bonus) * g


def _gdn_main(C, x, z, ab, tail, s_scr, prm):
    convw, alog, dtb, normw = prm
    t8 = tail[...]
    conv = x * convw[3:4, :]
    for s in (1, 2, 3):
        conv = conv + _shift_rows(x, t8, s, C) * convw[3 - s:4 - s, :]
    tail[...] = x[C - TAIL:, :]
    act = conv * _sigmoid(conv)
    q, k, v = act[:, 0:GD_W], act[:, GD_W:2 * GD_W], act[:, 2 * GD_W:3 * GD_W]
    yield

    g_all = -jnp.exp(alog[...]) * _softplus(ab + dtb[...])
    beta_all = _sigmoid(ab)
    tril, strict, eye = _tri_masks(C)
    gc = _cumsum_rows(g_all, tril.astype(BF16))
    gr = _cols_to_rows(gc, 16)
    eg = jnp.exp(gc)
    yield

    heads = range(GD_HEADS)
    sls = [slice(GD_HD * h, GD_HD * (h + 1)) for h in heads]
    qs = [q[:, sl] for sl in sls]
    qs = [t * (lax.rsqrt(jnp.sum(t * t, -1, keepdims=True) + 1e-6) * GD_HD ** -0.5) for t in qs]
    ks = [k[:, sl] for sl in sls]
    ks = [t * lax.rsqrt(jnp.sum(t * t, -1, keepdims=True) + 1e-6) for t in ks]
    yield
    betas = [beta_all[:, GD_HEADS + h:GD_HEADS + h + 1] for h in heads]
    decays = [jnp.where(tril, jnp.exp(jnp.where(tril, gc[:, h:h + 1] - gr[h:h + 1, :], 0.0)), 0.0) for h in heads]
    yield
    scs = [_bdot_nt(jnp.concatenate([k_h, q_h], axis=0), k_h) for k_h, q_h in zip(ks, qs)]
    yield
    invs = yield from _unit_lower_inverse(
        [jnp.where(strict, beta * sc[:C] * decay, 0.0) for beta, sc, decay in zip(betas, scs, decays)], eye, C)
    rhss = [jnp.concatenate([v[:, sl] * beta, k_h * (beta * eg[:, h:h + 1])], axis=1)
            for h, sl, beta, k_h in zip(heads, sls, betas, ks)]
    sols = [_bdot(t, rhs) for t, rhs in zip(invs, rhss)]
    yield
    s0s = [s_scr[h] for h in heads]
    pss = [_bdot_nt(jnp.concatenate([sol[:, GD_HD:], q_h], axis=0), s0) for sol, q_h, s0 in zip(sols, qs, s0s)]
    yield
    us = [sol[:, :GD_HD] - ps[:C] for sol, ps in zip(sols, pss)]
    os_ = [eg[:, h:h + 1] * ps[C:] + _bdot(sc[C:] * decay, u) for h, ps, sc, decay, u in zip(heads, pss, scs, decays, us)]
    yield
    for h in heads:
        gl = gc[C - 1:C, h:h + 1]
        s_scr[h] = jnp.exp(gl) * s0s[h] + _bdot_tn(us[h] * jnp.exp(gl - gc[:, h:h + 1]), ks[h])
    yield
    outs = [o * lax.rsqrt(jnp.mean(o * o, -1, keepdims=True) + NORM_EPS) * normw[...] for o in os_]
    return jnp.concatenate(outs, axis=1) * (z * _sigmoid(z))


def _mlstm_main(C, q, k, v, og, gates, c_scr, n_scr, m_scr, prm):
    ib, fb, normw = prm
    li_all = gates + ib[...]
    x = gates + fb[...]
    lf_all = jnp.minimum(x, 0.0) - jnp.log(1.0 + jnp.exp(-jnp.abs(x)))
    tril, _, _ = _tri_masks(C)
    fc = _cumsum_rows(lf_all, tril.astype(BF16))
    fr = _cols_to_rows(fc, 8)
    lir = _cols_to_rows(li_all, 8)
    lane = lax.broadcasted_iota(jnp.int32, (1, LANE), 1)
    m_all = m_scr[...]
    m_new = m_all
    yield

    heads = range(ML_HEADS)
    qs = [q[:, ML_DK * h:ML_DK * (h + 1)] * ML_DK ** -0.5 for h in heads]
    ks = [k[:, ML_DK * h:ML_DK * (h + 1)] for h in heads]
    vs = [v[:, ML_DV * h:ML_DV * (h + 1)] for h in heads]
    c0s = [c_scr[h] for h in heads]
    n0s = [n_scr[h:h + 1, :] for h in heads]
    qks = [_bdot_nt(q_h, k_h) for q_h, k_h in zip(qs, ks)]
    qcs = [_bdot_nt(q_h, c0) for q_h, c0 in zip(qs, c0s)]
    yield
    fcs = [fc[:, ML_HEADS + h:ML_HEADS + h + 1] for h in heads]
    m0s = [m_all[:, h:h + 1] for h in heads]
    dmats = [jnp.where(tril, fc_h - fr[ML_HEADS + h:ML_HEADS + h + 1, :] + lir[h:h + 1, :], -jnp.inf)
             for h, fc_h in zip(heads, fcs)]
    inters = [fc_h + m0 for fc_h, m0 in zip(fcs, m0s)]
    ms = [jnp.maximum(inter, jnp.max(dmat, -1, keepdims=True)) for inter, dmat in zip(inters, dmats)]
    yield
    ss = [qk * jnp.exp(dmat - m) for qk, dmat, m in zip(qks, dmats, ms)]
    wis = [jnp.exp(inter - m) for inter, m in zip(inters, ms)]
    yield
    nums = [wi * qc + _bdot(s, v_h) for wi, qc, s, v_h in zip(wis, qcs, ss, vs)]
    dens = [wi * jnp.sum(q_h * n0, -1, keepdims=True) + jnp.sum(s, -1, keepdims=True)
            for wi, q_h, n0, s in zip(wis, qs, n0s, ss)]
    yield
    hhs = [num / jnp.maximum(jnp.abs(den), jnp.exp(-m)) for num, den, m in zip(nums, dens, ms)]
    m_cs = [m[C - 1:C, :] for m in ms]
    wends = [jnp.exp(fc_h[C - 1:C, :] - fc_h + li_all[:, h:h + 1] - m_c) for h, fc_h, m_c in zip(heads, fcs, m_cs)]
    dstates = [jnp.exp(fc_h[C - 1:C, :] + m0 - m_c) for fc_h, m0, m_c in zip(fcs, m0s, m_cs)]
    for h in heads:
        c_scr[h] = dstates[h] * c0s[h] + _bdot_tn(vs[h] * wends[h], ks[h])
        n_scr[h:h + 1, :] = dstates[h] * n0s[h] + jnp.sum(wends[h] * ks[h], 0, keepdims=True)
        m_new = jnp.where(lane == h, m_cs[h], m_new)
    m_scr[...] = m_new
    yield
    outs = [hh * lax.rsqrt(jnp.mean(hh * hh, -1, keepdims=True) + NORM_EPS) for hh in hhs]
    return jnp.concatenate(outs, axis=1) * normw[...] * _sigmoid(og)


N_MIX_STATES = 7
N_RW_PRM, N_GD_PRM, N_ML_PRM = 11, 4, 3


def _mixer_body(C, nblk, p_ref, *refs):
    ins, refs = refs[:N_MIX_STATES], refs[N_MIX_STATES:]
    rw_prm, refs = refs[:N_RW_PRM], refs[N_RW_PRM:]
    gd_prm, refs = refs[:N_GD_PRM], refs[N_GD_PRM:]
    ml_prm, refs = refs[:N_ML_PRM], refs[N_ML_PRM:]
    o_ref, refs = refs[1], refs[2:]
    outs, scr = refs[:N_MIX_STATES], refs[N_MIX_STATES:]
    carry, s_rw, tail, s_gd, c_scr, n_scr, m_scr = scr
    i = pl.program_id(1)

    @pl.when(i == 0)
    def _():
        for dst, src in zip(scr, ins):
            dst[...] = src[0]

    pm, pt = p_ref[:, 0:P_MAIN], p_ref[:, P_MAIN:P_MAIN + P_TAIL]
    o_rw, o_gd, o_ml = _interleave([
        _rwkv_main(C, pm[:, 0:RW_P], carry, s_rw, rw_prm),
        _gdn_main(C, pm[:, RW_P:P_MAIN], pt[:, T_Z:T_Z + GD_W], pt[:, T_AB:T_AB + LANE], tail, s_gd, gd_prm),
        _mlstm_main(C, pt[:, T_MLQ:T_MLQ + ML_QK], pt[:, T_MLK:T_MLK + ML_QK], pt[:, T_MLV:T_MLV + ML_W],
                    pt[:, T_MLO:T_MLO + ML_W], pt[:, T_IF:T_IF + LANE], c_scr, n_scr, m_scr, ml_prm)])
    o_ref[:, 0:RW_W] = o_rw.astype(o_ref.dtype)
    o_ref[:, RW_W:RW_W + GD_W] = o_gd.astype(o_ref.dtype)
    o_ref[:, RW_W + GD_W:] = o_ml.astype(o_ref.dtype)

    @pl.when(i == nblk - 1)
    def _():
        outs[0][0] = pm[C - 1:C, 0:RW_P]
        for dst, src in zip(outs[1:], scr[1:]):
            dst[0] = src[...]


def _mixer_call(p, buf, seg, states, params):
    scratch = [pltpu.VMEM((1, RW_P), F32), pltpu.VMEM((RW_HEADS, RW_HD, RW_HD), F32),
               pltpu.VMEM((TAIL, 3 * GD_W), F32), pltpu.VMEM((GD_HEADS, GD_HD, GD_HD), F32),
               pltpu.VMEM((ML_HEADS, ML_DV, ML_DK), F32), pltpu.VMEM((ML_HEADS, ML_DK), F32), pltpu.VMEM((1, LANE), F32)]
    outs = _seq_call(_mixer_body, seg, [(p, P_MAIN + P_TAIL, 0)], list(states), params,
                     buf, D_MODEL, 0, scratch, "mixers")
    return outs[0], list(outs[1:])


def _ffn_act_body(C, nblk, g_ref, u_ref, tail_in, convw, convb, _buf, o_ref, tail_out, tail):
    i = pl.program_id(1)

    @pl.when(i == 0)
    def _():
        tail[...] = tail_in[0]

    x = g_ref[...]
    t8 = tail[...]
    conv = x * convw[2:3, :] + convb[...]
    for s in (1, 2):
        conv = conv + _shift_rows(x, t8, s, C) * convw[2 - s:3 - s, :]
    tail[...] = x[C - TAIL:, :]
    o_ref[...] = (conv * _sigmoid(conv) * u_ref[...]).astype(o_ref.dtype)

    @pl.when(i == nblk - 1)
    def _():
        tail_out[0] = tail[...]


def _ffn_act_call(g, u, buf, seg, tail, params, in_row_base=None):
    outs = _seq_call(
        _ffn_act_body, seg, [(g, D_FF, 0), (u, D_FF, 0)], [tail], params, buf, D_FF, 0,
        [pltpu.VMEM((TAIL, D_FF), F32)], "ffn_act", in_row_base)
    return outs[0], outs[1]


def _ffn_up_body(tm, h_ref, ssq_ref, wg_ref, wu_ref, convw, convb, o_ref, gt_ref, ut_ref, carry):
    i, j = pl.program_id(0), pl.program_id(1)
    h = h_ref[...]
    scale = _row_scale(ssq_ref)
    g = scale * jnp.dot(h, wg_ref[...].astype(BF16), preferred_element_type=F32)
    u = scale * jnp.dot(h, wu_ref[...].astype(BF16), preferred_element_type=F32)
    gt_ref[...] = g[tm - FFN_TAIL_ROWS:, :]
    ut_ref[...] = u[tm - FFN_TAIL_ROWS:, :]
    rows = i * tm + lax.broadcasted_iota(jnp.int32, (tm, 1), 0)
    g = jnp.where(rows >= ROW_META, g, 0.0)

    @pl.when(i == 0)
    def _():
        carry[j] = jnp.zeros((TAIL, COL_TILE), F32)

    t8 = carry[j]
    conv = g * convw[2:3, :] + convb[...]
    for s in (1, 2):
        conv = conv + _shift_rows(g, t8, s, tm) * convw[2 - s:3 - s, :]
    carry[j] = g[tm - TAIL:, :]
    o_ref[...] = (conv * _sigmoid(conv) * u).astype(o_ref.dtype)


def _ffn_up(hb, ssq, w_gate, w_up, layer, convw, convb, tm):
    m, k = hb.shape
    n = w_gate.shape[2]
    w_spec = pl.BlockSpec((None, k, COL_TILE), lambda i, j: (layer, 0, j))
    t_spec = pl.BlockSpec((FFN_TAIL_ROWS, COL_TILE), lambda i, j: (i, j))
    return pl.pallas_call(
        functools.partial(_ffn_up_body, tm), grid=(m // tm, n // COL_TILE),
        in_specs=[pl.BlockSpec((tm, k), lambda i, j: (i, 0), pipeline_mode=pl.Buffered(1)),
                  pl.BlockSpec((tm, LANE), lambda i, j: (i, 0)), w_spec, w_spec,
                  pl.BlockSpec((3, COL_TILE), lambda i, j: (0, j)), pl.BlockSpec((1, COL_TILE), lambda i, j: (0, j))],
        out_specs=[pl.BlockSpec((tm, COL_TILE), lambda i, j: (i, j)), t_spec, t_spec],
        out_shape=[jax.ShapeDtypeStruct((m, n), BF16), jax.ShapeDtypeStruct((m // tm * FFN_TAIL_ROWS, n), F32),
                   jax.ShapeDtypeStruct((m // tm * FFN_TAIL_ROWS, n), F32)],
        scratch_shapes=[pltpu.VMEM((n // COL_TILE, TAIL, COL_TILE), F32)],
        compiler_params=_cparams(2), name="ffn_up_act",
    )(hb, ssq, w_gate, w_up, convw, convb)


def _pad_cols(x, width):
    return jnp.pad(x, ((0, 0),) * (x.ndim - 1) + ((0, width - x.shape[-1]),))


def _lane_vec(x, start=0):
    return jnp.zeros((1, LANE), F32).at[0, start:start + x.shape[0]].set(x)


def _tail_cols(w):
    ml0 = RW_P + GD_P
    gd_ab = w[..., P_MAIN:P_MAIN + 2 * GD_HEADS]
    ml_if = w[..., ml0 + 2 * ML_QK + ML_W:ml0 + 2 * ML_QK + ML_W + 2 * ML_HEADS]
    cols = [_pad_cols(gd_ab, LANE), _pad_cols(ml_if, LANE), w[..., ml0:ml0 + 2 * ML_QK],
            w[..., P_MAIN + 2 * GD_HEADS:ml0], w[..., ml0 + 2 * ML_QK:ml0 + 2 * ML_QK + ML_W],
            w[..., ml0 + 2 * ML_QK + ML_W + 2 * ML_HEADS:ml0 + ML_P]]
    return jnp.concatenate(cols, axis=-1)


def _pad_tail(buf, width):
    b, r, w = buf.shape
    return jnp.pad(buf, ((0, 0), (TAIL - r, 0), (0, width - w)))


def _layer(x, xw, ssq, states, lp, wts, layer, segs):
    n = x.shape[0]
    p = _matmul_wide(xw, ssq, wts["w_in"], layer, name="proj_in")
    mix = jnp.zeros((n, D_MODEL), BF16)
    new_states = []
    for seg, (mix_st, ffn_tail) in zip(segs, states):
        if mix_st is None:
            mix_st = new_states[-1][0]
        mix, mix_new = _mixer_call(p, mix, seg, mix_st, lp["mixers"])
        new_states.append([mix_new, ffn_tail])
    x, xw, ssq = _matmul(mix, wts["w_out"], layer, ROW_TILE, x, norm_w=lp["norm_ffn_w"], name="proj_out")
    act, g_tail, u_tail = _ffn_up(xw, ssq, wts["w_gate"], wts["w_up"], layer, lp["ffn"][0], lp["ffn"][1], ROW_TILE)
    tail_base = n - g_tail.shape[0]
    for idx, seg in enumerate(segs):
        if idx == 0 or states[idx][0] is None:
            end = seg[0] + seg[1] * seg[2] * seg[3]
            kept = end - TAIL >= n - FFN_TAIL_ROWS
            new_states[idx][1] = g_tail[end - TAIL - tail_base:end - tail_base][None] if kept else None
        else:
            assert seg[0] >= n - FFN_TAIL_ROWS
            act, new_states[idx][1] = _ffn_act_call(g_tail, u_tail, act, seg, new_states[idx][1], lp["ffn"],
                                                    in_row_base=seg[0] - tail_base)
    if lp["next_norm_w"] is None:
        return _matmul(act, wts["w_down"], layer, ROW_TILE_DOWN, x, name="ffn_down"), None, None, new_states
    x, xw, ssq = _matmul(act, wts["w_down"], layer, ROW_TILE_DOWN, x, norm_w=lp["next_norm_w"], name="ffn_down")
    return x, xw, ssq, new_states


def kernel(x_prompt, x_sample, state_rwkv_wkv, state_rwkv_shift, state_gdn, cache_gdn_conv, state_mlstm_c, state_mlstm_n, state_mlstm_m, cache_ffn_conv, meta_tokens, norm_mix_w, w_in, rwkv_mu, rwkv_w0, rwkv_w_up, rwkv_a0, rwkv_a_up, rwkv_g_up, rwkv_k_k, rwkv_k_a, rwkv_r_k, rwkv_ln_w, rwkv_ln_b, gdn_conv_w, gdn_a_log, gdn_dt_bias, gdn_norm_w, mlstm_i_b, mlstm_f_b, mlstm_norm_w, w_out, norm_ffn_w, ffn_w_gate, ffn_w_up, ffn_conv_w, ffn_conv_b, ffn_w_down, final_norm_w):
    depth = w_in.shape[0]
    seq = x_prompt.shape[1]
    dec_b, dec_seq = x_sample.shape[0], x_sample.shape[1]
    row_sample = ROW_PROMPT + seq
    n_rows = row_sample + dec_b * dec_seq
    n_pad = -(-n_rows // ROW_TILE) * ROW_TILE
    segs = ((ROW_META, N_META, 1, 1), (ROW_PROMPT, CHUNK, 1, seq // CHUNK), (row_sample, dec_seq, dec_b, 1))

    x = jnp.concatenate([
        jnp.zeros((ROW_META, D_MODEL), F32), meta_tokens.astype(F32), x_prompt[0],
        x_sample.reshape(dec_b * dec_seq, D_MODEL), jnp.zeros((n_pad - n_rows, D_MODEL), F32)], axis=0)

    row = lambda a: a.reshape(1, -1)
    zero_mix = [jnp.zeros((1, 1, RW_P), F32), jnp.zeros((1, RW_HEADS, RW_HD, RW_HD), F32),
                jnp.zeros((1, TAIL, 3 * GD_W), F32), jnp.zeros((1, GD_HEADS, GD_HD, GD_HD), F32),
                jnp.zeros((1, ML_HEADS, ML_DV, ML_DK), F32), jnp.zeros((1, ML_HEADS, ML_DK), F32),
                jnp.zeros((1, 1, LANE), F32)]

    wts = {
        "w_in": jnp.concatenate([w_in[..., :P_MAIN], _tail_cols(w_in)], axis=-1).astype(BF16),
        "w_out": w_out,
        "w_gate": ffn_w_gate,
        "w_up": ffn_w_up,
        "w_down": ffn_w_down.astype(BF16),
    }

    xw, ssq = _prenorm(x, norm_mix_w[0], ROW_TILE_SMALL)
    p_out, s_out = [], []
    for l in range(depth):
        lp = {
            "next_norm_w": norm_mix_w[l + 1] if l + 1 < depth else None,
            "mixers": [row(rwkv_mu[l]), row(rwkv_w0[l]), rwkv_w_up[l], row(rwkv_a0[l]), rwkv_a_up[l], rwkv_g_up[l],
                       row(rwkv_k_k[l]), row(rwkv_k_a[l]), row(rwkv_r_k[l]), row(rwkv_ln_w[l]), row(rwkv_ln_b[l]),
                       gdn_conv_w[l], _lane_vec(gdn_a_log[l]), _lane_vec(gdn_dt_bias[l]), row(gdn_norm_w[l]),
                       _lane_vec(mlstm_i_b[l]), _lane_vec(mlstm_f_b[l], ML_HEADS), row(mlstm_norm_w[l])],
            "norm_ffn_w": norm_ffn_w[l],
            "ffn": [ffn_conv_w[l], row(ffn_conv_b[l])],
        }
        sample_mix = [state_rwkv_shift[l][:, None, :], state_rwkv_wkv[l],
                      _pad_tail(cache_gdn_conv[l], 3 * GD_W), state_gdn[l],
                      state_mlstm_c[l], state_mlstm_n[l], _pad_cols(state_mlstm_m[l], LANE)[:, None, :]]
        states = ([zero_mix, None], [None, None], [sample_mix, _pad_tail(cache_ffn_conv[l], D_FF)])
        x, xw, ssq, st = _layer(x, xw, ssq, states, lp, wts, l, segs)
        p_out.append(st[1])
        s_out.append(st[2])

    y_prompt = _rmsnorm(x, final_norm_w, F32, seq, ROW_PROMPT, CHUNK).reshape(1, seq, D_MODEL)
    y_sample = _rmsnorm(x, final_norm_w, F32, dec_b * dec_seq, row_sample, CHUNK).reshape(dec_b, dec_seq, D_MODEL)

    def collect(sts):
        stack = lambda f: jnp.stack([f(st) for st in sts], 0)
        return (stack(lambda st: st[0][1]), stack(lambda st: st[0][0][:, 0, :]),
                stack(lambda st: st[0][3]), stack(lambda st: st[0][2][:, TAIL - 3:, :]),
                stack(lambda st: st[0][4]), stack(lambda st: st[0][5]), stack(lambda st: st[0][6][:, 0, :ML_HEADS]),
                stack(lambda st: st[1][:, TAIL - 2:, :]))

    return (y_prompt, y_sample) + collect(p_out) + collect(s_out)
```

```python
import functools
import math

import jax
import jax.numpy as jnp
from jax import lax
from jax.experimental import pallas as pl
from jax.experimental.pallas import tpu as pltpu

F32 = jnp.float32
BF16 = jnp.bfloat16

D_MODEL = 4096
N_META = 16
CHUNK = 64
NORM_EPS = 1e-6
RW_HEADS, RW_HD = 24, 64
RW_W = RW_HEADS * RW_HD
RW_P = 3 * RW_W + 64 + 64 + 128
RW_GN_EPS = 64e-5
GD_HEADS, GD_HD = 12, 128
GD_W = GD_HEADS * GD_HD
GD_P = 4 * GD_W + 2 * GD_HEADS
ML_HEADS, ML_DK, ML_DV = 4, 128, 256
ML_W = ML_HEADS * ML_DV
ML_QK = ML_HEADS * ML_DK
ML_P = 2 * ML_QK + 2 * ML_W + 2 * ML_HEADS
D_FF = 11008

LANE = 128
TAIL = 8
ROW_META = 48
ROW_PROMPT = 64
ROW_TILE = 2128
ROW_TILE_DOWN = 1216
ROW_TILE_SMALL = 448
COL_TILE = 256
FFN_TAIL_ROWS = 512
VMEM_LIMIT = 56 * 1024 * 1024

P_MAIN = RW_P + 3 * GD_W
T_AB = 0
T_IF = 128
T_MLQ = 256
T_MLK = 768
T_Z = 1280
T_MLV = 2816
T_MLO = 3840
P_TAIL = 4864


def _cparams(n_axes):
    return pltpu.CompilerParams(dimension_semantics=("arbitrary",) * n_axes, vmem_limit_bytes=VMEM_LIMIT)


def _bdot(a, b):
    return jnp.dot(a.astype(BF16), b.astype(BF16), preferred_element_type=F32)


def _bdot_nt(a, b):
    return lax.dot_general(a.astype(BF16), b.astype(BF16), (((1,), (1,)), ((), ())), preferred_element_type=F32)


def _bdot_tn(a, b):
    return lax.dot_general(a.astype(BF16), b.astype(BF16), (((0,), (0,)), ((), ())), preferred_element_type=F32)


def _split3(x):
    hi = x.astype(BF16)
    r1 = x - hi.astype(F32)
    mid = r1.astype(BF16)
    lo = (r1 - mid.astype(F32)).astype(BF16)
    return hi, mid, lo


def _cumsum_rows(x, tril_bf):
    hi, mid, lo = _split3(x)
    d = lambda t: jnp.dot(tril_bf, t, preferred_element_type=F32)
    return d(hi) + d(mid) + d(lo)


def _cols_to_rows(x, n_rows):
    sel = (lax.broadcasted_iota(jnp.int32, (n_rows, LANE), 0) == lax.broadcasted_iota(jnp.int32, (n_rows, LANE), 1)).astype(BF16)
    hi, mid, lo = _split3(x)
    d = lambda t: lax.dot_general(sel, t, (((1,), (1,)), ((), ())), preferred_element_type=F32)
    return d(hi) + d(mid) + d(lo)


def _head_selectors(width, hd):
    sh = int(math.log2(hd))
    e = (lax.broadcasted_iota(jnp.int32, (width, LANE), 0) >> sh) == lax.broadcasted_iota(jnp.int32, (width, LANE), 1)
    et = lax.broadcasted_iota(jnp.int32, (LANE, width), 0) == (lax.broadcasted_iota(jnp.int32, (LANE, width), 1) >> sh)
    return e.astype(BF16), et.astype(BF16)


def _head_sums(x, sel):
    e, et = sel
    rows = x.shape[0]
    hi, lo, _ = _split3(x)
    part = jnp.dot(jnp.concatenate([hi, lo], axis=0), e, preferred_element_type=F32)
    hi, lo, _ = _split3(part[:rows] + part[rows:])
    back = jnp.dot(jnp.concatenate([hi, lo], axis=0), et, preferred_element_type=F32)
    return back[:rows] + back[rows:]


def _tri_masks(C):
    ri = lax.broadcasted_iota(jnp.int32, (C, C), 0)
    ci = lax.broadcasted_iota(jnp.int32, (C, C), 1)
    return ri >= ci, ri > ci, ri == ci


def _unit_lower_inverse(As, eye, C):
    Ps = [-A for A in As]
    Ts = [jnp.where(eye, 1.0, P) for P in Ps]
    levels = int(math.log2(C))
    if levels > 1:
        Ps = [_bdot(P, P) for P in Ps]
        yield
    for k in range(1, levels):
        if k < levels - 1:
            Ys = [_bdot(jnp.concatenate([T, P], axis=0), P) for T, P in zip(Ts, Ps)]
            Ts = [T + Y[:C] for T, Y in zip(Ts, Ys)]
            Ps = [Y[C:] for Y in Ys]
        else:
            Ts = [T + _bdot(T, P) for T, P in zip(Ts, Ps)]
        yield
    return Ts


def _interleave(gens):
    results, alive = [None] * len(gens), list(range(len(gens)))
    while alive:
        for idx in list(alive):
            try:
                next(gens[idx])
            except StopIteration as stop:
                results[idx] = stop.value
                alive.remove(idx)
    return results


def _sigmoid(x):
    return jax.nn.sigmoid(x)


def _softplus(x):
    return jnp.maximum(x, 0.0) + jnp.log(1.0 + jnp.exp(-jnp.abs(x)))


def _shift_rows(x, tail, s, C):
    xr = pltpu.roll(x, s, 0)
    r8 = lax.broadcasted_iota(jnp.int32, (TAIL, 1), 0)
    top = jnp.where(r8 < s, pltpu.roll(tail, s, 0), xr[0:TAIL])
    return jnp.concatenate([top, xr[TAIL:]], axis=0)


def _rmsnorm_body(x_ref, w_ref, o_ref):
    x = x_ref[...]
    y = x * lax.rsqrt(jnp.mean(x * x, -1, keepdims=True) + NORM_EPS)
    o_ref[...] = (y * w_ref[...]).astype(o_ref.dtype)


def _rmsnorm(x, w, out_dtype, rows, row_base, block):
    d = x.shape[1]
    rb = row_base // block
    return pl.pallas_call(
        _rmsnorm_body,
        grid=(rows // block,),
        in_specs=[pl.BlockSpec((block, d), lambda i: (rb + i, 0)), pl.BlockSpec((1, d), lambda i: (0, 0))],
        out_specs=pl.BlockSpec((block, d), lambda i: (i, 0)),
        out_shape=jax.ShapeDtypeStruct((rows, d), out_dtype),
        compiler_params=_cparams(1),
        name="rmsnorm",
    )(x, w.reshape(1, d))


def _lane_fold(x):
    return functools.reduce(lambda a, b: a + b, [x[:, c:c + LANE] for c in range(0, x.shape[1], LANE)])


def _row_scale(ssq_ref):
    return lax.rsqrt(jnp.sum(ssq_ref[...], -1, keepdims=True) * (1.0 / D_MODEL) + NORM_EPS)


def _prenorm_body(x_ref, w_ref, xw_ref, ssq_ref):
    x = x_ref[...]
    xw_ref[...] = (x * w_ref[...]).astype(xw_ref.dtype)
    ssq_ref[...] = _lane_fold(x * x)


def _prenorm(x, w, block):
    n, d = x.shape
    return pl.pallas_call(
        _prenorm_body, grid=(n // block,),
        in_specs=[pl.BlockSpec((block, d), lambda i: (i, 0)), pl.BlockSpec((1, d), lambda i: (0, 0))],
        out_specs=[pl.BlockSpec((block, d), lambda i: (i, 0)), pl.BlockSpec((block, LANE), lambda i: (i, 0))],
        out_shape=[jax.ShapeDtypeStruct((n, d), BF16), jax.ShapeDtypeStruct((n, LANE), F32)],
        compiler_params=_cparams(1), name="prenorm",
    )(x, w.reshape(1, d))


def _mm_scaled_body(a_ref, ssq_ref, w_ref, o_ref):
    o_ref[...] = _row_scale(ssq_ref) * jnp.dot(a_ref[...], w_ref[...].astype(BF16), preferred_element_type=F32)


def _mm_res_norm_body(a_ref, w_ref, r_ref, nw_ref, o_ref, xw_ref, ssq_ref):
    j = pl.program_id(1)
    acc = r_ref[...] + jnp.dot(a_ref[...], w_ref[...].astype(BF16), preferred_element_type=F32)
    o_ref[...] = acc
    xw_ref[...] = (acc * nw_ref[...]).astype(xw_ref.dtype)
    part = _lane_fold(acc * acc)

    @pl.when(j == 0)
    def _():
        ssq_ref[...] = part

    @pl.when(j > 0)
    def _():
        ssq_ref[...] += part


def _mm_res_body(a_ref, w_ref, r_ref, o_ref):
    o_ref[...] = r_ref[...] + jnp.dot(a_ref[...], w_ref[...].astype(BF16), preferred_element_type=F32)


def _matmul_wide(a, ssq, w, layer, name="matmul_wide"):
    m, k = a.shape
    n = w.shape[2]
    tm, tn = ROW_TILE_DOWN // 2, 4 * COL_TILE
    return pl.pallas_call(
        _mm_scaled_body, grid=(n // tn, m // tm),
        in_specs=[pl.BlockSpec((tm, k), lambda j, i: (i, 0)), pl.BlockSpec((tm, LANE), lambda j, i: (i, 0)),
                  pl.BlockSpec((None, k, tn), lambda j, i: (layer, 0, j))],
        out_specs=pl.BlockSpec((tm, tn), lambda j, i: (i, j)),
        out_shape=jax.ShapeDtypeStruct((m, n), F32),
        compiler_params=_cparams(2), name=name,
    )(a, ssq, w)


def _matmul(a, w, layer, tm, res, norm_w=None, name="matmul"):
    m, k = a.shape
    n = w.shape[2]
    a_spec = pl.BlockSpec((tm, k), lambda i, j: (i, 0), pipeline_mode=pl.Buffered(1))
    w_spec = pl.BlockSpec((None, k, COL_TILE), lambda i, j: (layer, 0, j))
    o_spec = pl.BlockSpec((tm, COL_TILE), lambda i, j: (i, j))
    if norm_w is None:
        return pl.pallas_call(
            _mm_res_body, grid=(m // tm, n // COL_TILE), in_specs=[a_spec, w_spec, o_spec], out_specs=o_spec,
            out_shape=jax.ShapeDtypeStruct((m, n), F32),
            compiler_params=_cparams(2), name=name,
        )(a, w, res)
    return pl.pallas_call(
        _mm_res_norm_body, grid=(m // tm, n // COL_TILE),
        in_specs=[a_spec, w_spec, o_spec, pl.BlockSpec((1, COL_TILE), lambda i, j: (0, j))],
        out_specs=[o_spec, o_spec, pl.BlockSpec((tm, LANE), lambda i, j: (i, 0))],
        out_shape=[jax.ShapeDtypeStruct((m, n), F32), jax.ShapeDtypeStruct((m, n), BF16),
                   jax.ShapeDtypeStruct((m, LANE), F32)],
        compiler_params=_cparams(2), name=name,
    )(a, w, res, norm_w.reshape(1, n))


def _row_spec(width, col_start, seg):
    row_base, C, B, nblk = seg
    rb, cb = row_base // C, col_start // width
    return pl.BlockSpec((C, width), lambda b, i: (rb + b * nblk + i, cb))


def _state_spec(shape):
    nd = len(shape)
    return pl.BlockSpec((1,) + tuple(shape[1:]), lambda b, i: (b,) + (0,) * (nd - 1))


def _param_spec(shape):
    nd = len(shape)
    return pl.BlockSpec(tuple(shape), lambda b, i: (0,) * nd)


def _seq_call(body, seg, row_ins, state_ins, params, buf, out_width, out_col, scratch, name, in_row_base=None):
    row_base, C, B, nblk = seg
    in_seg = seg if in_row_base is None else (in_row_base, C, B, nblk)
    in_specs = [_row_spec(w, c, in_seg) for (_, w, c) in row_ins]
    in_specs += [_state_spec(s.shape) for s in state_ins]
    in_specs += [_param_spec(p.shape) for p in params]
    in_specs += [pl.BlockSpec(memory_space=pl.ANY)]
    args = [a for (a, _, _) in row_ins] + list(state_ins) + list(params) + [buf]
    out_specs = [_row_spec(out_width, out_col, seg)] + [_state_spec(s.shape) for s in state_ins]
    out_shape = [jax.ShapeDtypeStruct(buf.shape, buf.dtype)] + [jax.ShapeDtypeStruct(s.shape, s.dtype) for s in state_ins]
    return pl.pallas_call(
        functools.partial(body, C, nblk), grid=(B, nblk), in_specs=in_specs, out_specs=out_specs,
        out_shape=out_shape, scratch_shapes=scratch,
        input_output_aliases={len(args) - 1: 0},
        compiler_params=_cparams(2), name=name,
    )(*args)


def _rwkv_main(C, p, carry, s_scr, prm):
    mu, w0, wup, a0, aup, gup, kkw, kaw, rkw, lnw, lnb = prm
    rows = lax.broadcasted_iota(jnp.int32, (C, 1), 0)
    prev = jnp.where(rows == 0, carry[...], pltpu.roll(p, 1, 0))
    carry[...] = p[C - 1:C, :]
    xs = p + (prev - p) * mu[...]
    r, k, v = xs[:, 0:RW_W], xs[:, RW_W:2 * RW_W], xs[:, 2 * RW_W:3 * RW_W]
    dw, da, dg = xs[:, 3 * RW_W:3 * RW_W + 64], xs[:, 3 * RW_W + 64:3 * RW_W + 128], xs[:, 3 * RW_W + 128:RW_P]
    lw = -math.exp(-0.5) * _sigmoid(w0[...] + _bdot(jnp.tanh(dw), wup[...]))
    a = _sigmoid(a0[...] + _bdot(da, aup[...]))
    g = _bdot(_sigmoid(dg), gup[...])
    sel = _head_selectors(RW_W, RW_HD)
    kkx = k * kkw[...]
    kkn = kkx * lax.rsqrt(_head_sums(kkx * kkx, sel) + 1e-6)
    kt = k * (1.0 + (a - 1.0) * kaw[...])
    bonus = _head_sums(r * kt * rkw[...], sel) * v
    kb = kkn * a
    yield

    tril, strict, eye = _tri_masks(C)
    cl = _cumsum_rows(lw, tril.astype(BF16))
    w_in = jnp.exp(cl)
    w_ex = jnp.exp(cl - lw)
    w_inv = jnp.exp(-cl)
    w_end = jnp.exp(cl[C - 1:C, :] - cl)

    heads = range(RW_HEADS)
    sls = [slice(RW_HD * h, RW_HD * (h + 1)) for h in heads]
    kk_ex, r_in = kkn * w_ex, r * w_in
    kt_inv, kb_inv = kt * w_inv, kb * w_inv
    kt_end, kb_end = kt * w_end, kb * w_end
    yield
    lhss = [jnp.concatenate([kk_ex[:, sl], r_in[:, sl]], axis=0) for sl in sls]
    rhss = [jnp.concatenate([kt_inv[:, sl], kb_inv[:, sl]], axis=0) for sl in sls]
    scs = [_bdot_nt(lhs, rhs) for lhs, rhs in zip(lhss, rhss)]
    yield
    s0s = [s_scr[h] for h in heads]
    pss = [_bdot_nt(lhs, s0) for lhs, s0 in zip(lhss, s0s)]
    yield
    negs = [-(ps[:C] + _bdot(jnp.where(strict, sc[:C, :C], 0.0), v[:, sl])) for ps, sc, sl in zip(pss, scs, sls)]
    yield
    invs = yield from _unit_lower_inverse([jnp.where(strict, sc[:C, C:], 0.0) for sc in scs], eye, C)
    us = [_bdot(t, n) for t, n in zip(invs, negs)]
    yield
    vus = [jnp.concatenate([v[:, sl], u], axis=0) for u, sl in zip(us, sls)]
    a_rs = [jnp.concatenate([jnp.where(tril, sc[C:, :C], 0.0), jnp.where(tril, sc[C:, C:], 0.0)], axis=1) for sc in scs]
    ys = [ps[C:] + _bdot(a_r, vu) for ps, a_r, vu in zip(pss, a_rs, vus)]
    yield
    kes = [jnp.concatenate([kt_end[:, sl], kb_end[:, sl]], axis=0) for sl in sls]
    for h in heads:
        s_scr[h] = s0s[h] * w_in[C - 1:C, sls[h]] + _bdot_tn(vus[h], kes[h])
    yield
    y = jnp.concatenate(ys, axis=1)
    yc = y - _head_sums(y, sel) * (1.0 / RW_HD)
    yn = yc * lax.rsqrt(_head_sums(yc * yc, sel) * (1.0 / RW_HD) + RW_GN_EPS)
    return (yn * lnw[...] + lnb[...] + bonus) * g


def _gdn_main(C, x, z, ab, tail, s_scr, prm):
    convw, alog, dtb, normw = prm
    t8 = tail[...]
    conv = x * convw[3:4, :]
    for s in (1, 2, 3):
        conv = conv + _shift_rows(x, t8, s, C) * convw[3 - s:4 - s, :]
    tail[...] = x[C - TAIL:, :]
    act = conv * _sigmoid(conv)
    q, k, v = act[:, 0:GD_W], act[:, GD_W:2 * GD_W], act[:, 2 * GD_W:3 * GD_W]
    yield

    g_all = -jnp.exp(alog[...]) * _softplus(ab + dtb[...])
    beta_all = _sigmoid(ab)
    tril, strict, eye = _tri_masks(C)
    gc = _cumsum_rows(g_all, tril.astype(BF16))
    gr = _cols_to_rows(gc, 16)
    eg = jnp.exp(gc)
    yield

    heads = range(GD_HEADS)
    sls = [slice(GD_HD * h, GD_HD * (h + 1)) for h in heads]
    qs = [q[:, sl] for sl in sls]
    qs = [t * (lax.rsqrt(jnp.sum(t * t, -1, keepdims=True) + 1e-6) * GD_HD ** -0.5) for t in qs]
    ks = [k[:, sl] for sl in sls]
    ks = [t * lax.rsqrt(jnp.sum(t * t, -1, keepdims=True) + 1e-6) for t in ks]
    yield
    betas = [beta_all[:, GD_HEADS + h:GD_HEADS + h + 1] for h in heads]
    decays = [jnp.where(tril, jnp.exp(jnp.where(tril, gc[:, h:h + 1] - gr[h:h + 1, :], 0.0)), 0.0) for h in heads]
    yield
    scs = [_bdot_nt(jnp.concatenate([k_h, q_h], axis=0), k_h) for k_h, q_h in zip(ks, qs)]
    yield
    invs = yield from _unit_lower_inverse(
        [jnp.where(strict, beta * sc[:C] * decay, 0.0) for beta, sc, decay in zip(betas, scs, decays)], eye, C)
    rhss = [jnp.concatenate([v[:, sl] * beta, k_h * (beta * eg[:, h:h + 1])], axis=1)
            for h, sl, beta, k_h in zip(heads, sls, betas, ks)]
    sols = [_bdot(t, rhs) for t, rhs in zip(invs, rhss)]
    yield
    s0s = [s_scr[h] for h in heads]
    pss = [_bdot_nt(jnp.concatenate([sol[:, GD_HD:], q_h], axis=0), s0) for sol, q_h, s0 in zip(sols, qs, s0s)]
    yield
    us = [sol[:, :GD_HD] - ps[:C] for sol, ps in zip(sols, pss)]
    os_ = [eg[:, h:h + 1] * ps[C:] + _bdot(sc[C:] * decay, u) for h, ps, sc, decay, u in zip(heads, pss, scs, decays, us)]
    yield
    for h in heads:
        gl = gc[C - 1:C, h:h + 1]
        s_scr[h] = jnp.exp(gl) * s0s[h] + _bdot_tn(us[h] * jnp.exp(gl - gc[:, h:h + 1]), ks[h])
    yield
    outs = [o * lax.rsqrt(jnp.mean(o * o, -1, keepdims=True) + NORM_EPS) * normw[...] for o in os_]
    return jnp.concatenate(outs, axis=1) * (z * _sigmoid(z))


def _mlstm_main(C, q, k, v, og, gates, c_scr, n_scr, m_scr, prm):
    ib, fb, normw = prm
    li_all = gates + ib[...]
    x = gates + fb[...]
    lf_all = jnp.minimum(x, 0.0) - jnp.log(1.0 + jnp.exp(-jnp.abs(x)))
    tril, _, _ = _tri_masks(C)
    fc = _cumsum_rows(lf_all, tril.astype(BF16))
    fr = _cols_to_rows(fc, 8)
    lir = _cols_to_rows(li_all, 8)
    lane = lax.broadcasted_iota(jnp.int32, (1, LANE), 1)
    m_all = m_scr[...]
    m_new = m_all
    yield

    heads = range(ML_HEADS)
    qs = [q[:, ML_DK * h:ML_DK * (h + 1)] * ML_DK ** -0.5 for h in heads]
    ks = [k[:, ML_DK * h:ML_DK * (h + 1)] for h in heads]
    vs = [v[:, ML_DV * h:ML_DV * (h + 1)] for h in heads]
    c0s = [c_scr[h] for h in heads]
    n0s = [n_scr[h:h + 1, :] for h in heads]
    qks = [_bdot_nt(q_h, k_h) for q_h, k_h in zip(qs, ks)]
    qcs = [_bdot_nt(q_h, c0) for q_h, c0 in zip(qs, c0s)]
    yield
    fcs = [fc[:, ML_HEADS + h:ML_HEADS + h + 1] for h in heads]
    m0s = [m_all[:, h:h + 1] for h in heads]
    dmats = [jnp.where(tril, fc_h - fr[ML_HEADS + h:ML_HEADS + h + 1, :] + lir[h:h + 1, :], -jnp.inf)
             for h, fc_h in zip(heads, fcs)]
    inters = [fc_h + m0 for fc_h, m0 in zip(fcs, m0s)]
    ms = [jnp.maximum(inter, jnp.max(dmat, -1, keepdims=True)) for inter, dmat in zip(inters, dmats)]
    yield
    ss = [qk * jnp.exp(dmat - m) for qk, dmat, m in zip(qks, dmats, ms)]
    wis = [jnp.exp(inter - m) for inter, m in zip(inters, ms)]
    yield
    nums = [wi * qc + _bdot(s, v_h) for wi, qc, s, v_h in zip(wis, qcs, ss, vs)]
    dens = [wi * jnp.sum(q_h * n0, -1, keepdims=True) + jnp.sum(s, -1, keepdims=True)
            for wi, q_h, n0, s in zip(wis, qs, n0s, ss)]
    yield
    hhs = [num / jnp.maximum(jnp.abs(den), jnp.exp(-m)) for num, den, m in zip(nums, dens, ms)]
    m_cs = [m[C - 1:C, :] for m in ms]
    wends = [jnp.exp(fc_h[C - 1:C, :] - fc_h + li_all[:, h:h + 1] - m_c) for h, fc_h, m_c in zip(heads, fcs, m_cs)]
    dstates = [jnp.exp(fc_h[C - 1:C, :] + m0 - m_c) for fc_h, m0, m_c in zip(fcs, m0s, m_cs)]
    for h in heads:
        c_scr[h] = dstates[h] * c0s[h] + _bdot_tn(vs[h] * wends[h], ks[h])
        n_scr[h:h + 1, :] = dstates[h] * n0s[h] + jnp.sum(wends[h] * ks[h], 0, keepdims=True)
        m_new = jnp.where(lane == h, m_cs[h], m_new)
    m_scr[...] = m_new
    yield
    outs = [hh * lax.rsqrt(jnp.mean(hh * hh, -1, keepdims=True) + NORM_EPS) for hh in hhs]
    return jnp.concatenate(outs, axis=1) * normw[...] * _sigmoid(og)


N_MIX_STATES = 7
N_RW_PRM, N_GD_PRM, N_ML_PRM = 11, 4, 3


def _mixer_body(C, nblk, p_ref, *refs):
    ins, refs = refs[:N_MIX_STATES], refs[N_MIX_STATES:]
    rw_prm, refs = refs[:N_RW_PRM], refs[N_RW_PRM:]
    gd_prm, refs = refs[:N_GD_PRM], refs[N_GD_PRM:]
    ml_prm, refs = refs[:N_ML_PRM], refs[N_ML_PRM:]
    o_ref, refs = refs[1], refs[2:]
    outs, scr = refs[:N_MIX_STATES], refs[N_MIX_STATES:]
    carry, s_rw, tail, s_gd, c_scr, n_scr, m_scr = scr
    i = pl.program_id(1)

    @pl.when(i == 0)
    def _():
        for dst, src in zip(scr, ins):
            dst[...] = src[0]

    pm, pt = p_ref[:, 0:P_MAIN], p_ref[:, P_MAIN:P_MAIN + P_TAIL]
    o_rw, o_gd, o_ml = _interleave([
        _rwkv_main(C, pm[:, 0:RW_P], carry, s_rw, rw_prm),
        _gdn_main(C, pm[:, RW_P:P_MAIN], pt[:, T_Z:T_Z + GD_W], pt[:, T_AB:T_AB + LANE], tail, s_gd, gd_prm),
        _mlstm_main(C, pt[:, T_MLQ:T_MLQ + ML_QK], pt[:, T_MLK:T_MLK + ML_QK], pt[:, T_MLV:T_MLV + ML_W],
                    pt[:, T_MLO:T_MLO + ML_W], pt[:, T_IF:T_IF + LANE], c_scr, n_scr, m_scr, ml_prm)])
    o_ref[:, 0:RW_W] = o_rw.astype(o_ref.dtype)
    o_ref[:, RW_W:RW_W + GD_W] = o_gd.astype(o_ref.dtype)
    o_ref[:, RW_W + GD_W:] = o_ml.astype(o_ref.dtype)

    @pl.when(i == nblk - 1)
    def _():
        outs[0][0] = pm[C - 1:C, 0:RW_P]
        for dst, src in zip(outs[1:], scr[1:]):
            dst[0] = src[...]


def _mixer_call(p, buf, seg, states, params):
    scratch = [pltpu.VMEM((1, RW_P), F32), pltpu.VMEM((RW_HEADS, RW_HD, RW_HD), F32),
               pltpu.VMEM((TAIL, 3 * GD_W), F32), pltpu.VMEM((GD_HEADS, GD_HD, GD_HD), F32),
               pltpu.VMEM((ML_HEADS, ML_DV, ML_DK), F32), pltpu.VMEM((ML_HEADS, ML_DK), F32), pltpu.VMEM((1, LANE), F32)]
    outs = _seq_call(_mixer_body, seg, [(p, P_MAIN + P_TAIL, 0)], list(states), params,
                     buf, D_MODEL, 0, scratch, "mixers")
    return outs[0], list(outs[1:])


def _ffn_act_body(C, nblk, g_ref, u_ref, tail_in, convw, convb, _buf, o_ref, tail_out, tail):
    i = pl.program_id(1)

    @pl.when(i == 0)
    def _():
        tail[...] = tail_in[0]

    x = g_ref[...]
    t8 = tail[...]
    conv = x * convw[2:3, :] + convb[...]
    for s in (1, 2):
        conv = conv + _shift_rows(x, t8, s, C) * convw[2 - s:3 - s, :]
    tail[...] = x[C - TAIL:, :]
    o_ref[...] = (conv * _sigmoid(conv) * u_ref[...]).astype(o_ref.dtype)

    @pl.when(i == nblk - 1)
    def _():
        tail_out[0] = tail[...]


def _ffn_act_call(g, u, buf, seg, tail, params, in_row_base=None):
    outs = _seq_call(
        _ffn_act_body, seg, [(g, D_FF, 0), (u, D_FF, 0)], [tail], params, buf, D_FF, 0,
        [pltpu.VMEM((TAIL, D_FF), F32)], "ffn_act", in_row_base)
    return outs[0], outs[1]


def _ffn_up_body(tm, h_ref, ssq_ref, wg_ref, wu_ref, convw, convb, o_ref, gt_ref, ut_ref, carry):
    i, j = pl.program_id(0), pl.program_id(1)
    h = h_ref[...]
    scale = _row_scale(ssq_ref)
    g = scale * jnp.dot(h, wg_ref[...].astype(BF16), preferred_element_type=F32)
    u = scale * jnp.dot(h, wu_ref[...].astype(BF16), preferred_element_type=F32)
    gt_ref[...] = g[tm - FFN_TAIL_ROWS:, :]
    ut_ref[...] = u[tm - FFN_TAIL_ROWS:, :]
    rows = i * tm + lax.broadcasted_iota(jnp.int32, (tm, 1), 0)
    g = jnp.where(rows >= ROW_META, g, 0.0)

    @pl.when(i == 0)
    def _():
        carry[j] = jnp.zeros((TAIL, COL_TILE), F32)

    t8 = carry[j]
    conv = g * convw[2:3, :] + convb[...]
    for s in (1, 2):
        conv = conv + _shift_rows(g, t8, s, tm) * convw[2 - s:3 - s, :]
    carry[j] = g[tm - TAIL:, :]
    o_ref[...] = (conv * _sigmoid(conv) * u).astype(o_ref.dtype)


def _ffn_up(hb, ssq, w_gate, w_up, layer, convw, convb, tm):
    m, k = hb.shape
    n = w_gate.shape[2]
    w_spec = pl.BlockSpec((None, k, COL_TILE), lambda i, j: (layer, 0, j))
    t_spec = pl.BlockSpec((FFN_TAIL_ROWS, COL_TILE), lambda i, j: (i, j))
    return pl.pallas_call(
        functools.partial(_ffn_up_body, tm), grid=(m // tm, n // COL_TILE),
        in_specs=[pl.BlockSpec((tm, k), lambda i, j: (i, 0), pipeline_mode=pl.Buffered(1)),
                  pl.BlockSpec((tm, LANE), lambda i, j: (i, 0)), w_spec, w_spec,
                  pl.BlockSpec((3, COL_TILE), lambda i, j: (0, j)), pl.BlockSpec((1, COL_TILE), lambda i, j: (0, j))],
        out_specs=[pl.BlockSpec((tm, COL_TILE), lambda i, j: (i, j)), t_spec, t_spec],
        out_shape=[jax.ShapeDtypeStruct((m, n), BF16), jax.ShapeDtypeStruct((m // tm * FFN_TAIL_ROWS, n), F32),
                   jax.ShapeDtypeStruct((m // tm * FFN_TAIL_ROWS, n), F32)],
        scratch_shapes=[pltpu.VMEM((n // COL_TILE, TAIL, COL_TILE), F32)],
        compiler_params=_cparams(2), name="ffn_up_act",
    )(hb, ssq, w_gate, w_up, convw, convb)


def _pad_cols(x, width):
    return jnp.pad(x, ((0, 0),) * (x.ndim - 1) + ((0, width - x.shape[-1]),))


def _lane_vec(x, start=0):
    return jnp.zeros((1, LANE), F32).at[0, start:start + x.shape[0]].set(x)


def _tail_cols(w):
    ml0 = RW_P + GD_P
    gd_ab = w[..., P_MAIN:P_MAIN + 2 * GD_HEADS]
    ml_if = w[..., ml0 + 2 * ML_QK + ML_W:ml0 + 2 * ML_QK + ML_W + 2 * ML_HEADS]
    cols = [_pad_cols(gd_ab, LANE), _pad_cols(ml_if, LANE), w[..., ml0:ml0 + 2 * ML_QK],
            w[..., P_MAIN + 2 * GD_HEADS:ml0], w[..., ml0 + 2 * ML_QK:ml0 + 2 * ML_QK + ML_W],
            w[..., ml0 + 2 * ML_QK + ML_W + 2 * ML_HEADS:ml0 + ML_P]]
    return jnp.concatenate(cols, axis=-1)


def _pad_tail(buf, width):
    b, r, w = buf.shape
    return jnp.pad(buf, ((0, 0), (TAIL - r, 0), (0, width - w)))


def _layer(x, xw, ssq, states, lp, wts, layer, segs):
    n = x.shape[0]
    p = _matmul_wide(xw, ssq, wts["w_in"], layer, name="proj_in")
    mix = jnp.zeros((n, D_MODEL), BF16)
    new_states = []
    for seg, (mix_st, ffn_tail) in zip(segs, states):
        if mix_st is None:
            mix_st = new_states[-1][0]
        mix, mix_new = _mixer_call(p, mix, seg, mix_st, lp["mixers"])
        new_states.append([mix_new, ffn_tail])
    x, xw, ssq = _matmul(mix, wts["w_out"], layer, ROW_TILE, x, norm_w=lp["norm_ffn_w"], name="proj_out")
    act, g_tail, u_tail = _ffn_up(xw, ssq, wts["w_gate"], wts["w_up"], layer, lp["ffn"][0], lp["ffn"][1], ROW_TILE)
    tail_base = n - g_tail.shape[0]
    for idx, seg in enumerate(segs):
        if idx == 0 or states[idx][0] is None:
            end = seg[0] + seg[1] * seg[2] * seg[3]
            kept = end - TAIL >= n - FFN_TAIL_ROWS
            new_states[idx][1] = g_tail[end - TAIL - tail_base:end - tail_base][None] if kept else None
        else:
            assert seg[0] >= n - FFN_TAIL_ROWS
            act, new_states[idx][1] = _ffn_act_call(g_tail, u_tail, act, seg, new_states[idx][1], lp["ffn"],
                                                    in_row_base=seg[0] - tail_base)
    if lp["next_norm_w"] is None:
        return _matmul(act, wts["w_down"], layer, ROW_TILE_DOWN, x, name="ffn_down"), None, None, new_states
    x, xw, ssq = _matmul(act, wts["w_down"], layer, ROW_TILE_DOWN, x, norm_w=lp["next_norm_w"], name="ffn_down")
    return x, xw, ssq, new_states


def kernel(x_prompt, x_sample, state_rwkv_wkv, state_rwkv_shift, state_gdn, cache_gdn_conv, state_mlstm_c, state_mlstm_n, state_mlstm_m, cache_ffn_conv, meta_tokens, norm_mix_w, w_in, rwkv_mu, rwkv_w0, rwkv_w_up, rwkv_a0, rwkv_a_up, rwkv_g_up, rwkv_k_k, rwkv_k_a, rwkv_r_k, rwkv_ln_w, rwkv_ln_b, gdn_conv_w, gdn_a_log, gdn_dt_bias, gdn_norm_w, mlstm_i_b, mlstm_f_b, mlstm_norm_w, w_out, norm_ffn_w, ffn_w_gate, ffn_w_up, ffn_conv_w, ffn_conv_b, ffn_w_down, final_norm_w):
    depth = w_in.shape[0]
    seq = x_prompt.shape[1]
    dec_b, dec_seq = x_sample.shape[0], x_sample.shape[1]
    row_sample = ROW_PROMPT + seq
    n_rows = row_sample + dec_b * dec_seq
    n_pad = -(-n_rows // ROW_TILE) * ROW_TILE
    segs = ((ROW_META, N_META, 1, 1), (ROW_PROMPT, CHUNK, 1, seq // CHUNK), (row_sample, dec_seq, dec_b, 1))

    x = jnp.concatenate([
        jnp.zeros((ROW_META, D_MODEL), F32), meta_tokens.astype(F32), x_prompt[0],
        x_sample.reshape(dec_b * dec_seq, D_MODEL), jnp.zeros((n_pad - n_rows, D_MODEL), F32)], axis=0)

    row = lambda a: a.reshape(1, -1)
    zero_mix = [jnp.zeros((1, 1, RW_P), F32), jnp.zeros((1, RW_HEADS, RW_HD, RW_HD), F32),
                jnp.zeros((1, TAIL, 3 * GD_W), F32), jnp.zeros((1, GD_HEADS, GD_HD, GD_HD), F32),
                jnp.zeros((1, ML_HEADS, ML_DV, ML_DK), F32), jnp.zeros((1, ML_HEADS, ML_DK), F32),
                jnp.zeros((1, 1, LANE), F32)]

    wts = {
        "w_in": jnp.concatenate([w_in[..., :P_MAIN], _tail_cols(w_in)], axis=-1).astype(BF16),
        "w_out": w_out,
        "w_gate": ffn_w_gate,
        "w_up": ffn_w_up,
        "w_down": ffn_w_down.astype(BF16),
    }

    xw, ssq = _prenorm(x, norm_mix_w[0], ROW_TILE_SMALL)
    p_out, s_out = [], []
    for l in range(depth):
        lp = {
            "next_norm_w": norm_mix_w[l + 1] if l + 1 < depth else None,
            "mixers": [row(rwkv_mu[l]), row(rwkv_w0[l]), rwkv_w_up[l], row(rwkv_a0[l]), rwkv_a_up[l], rwkv_g_up[l],
                       row(rwkv_k_k[l]), row(rwkv_k_a[l]), row(rwkv_r_k[l]), row(rwkv_ln_w[l]), row(rwkv_ln_b[l]),
                       gdn_conv_w[l], _lane_vec(gdn_a_log[l]), _lane_vec(gdn_dt_bias[l]), row(gdn_norm_w[l]),
                       _lane_vec(mlstm_i_b[l]), _lane_vec(mlstm_f_b[l], ML_HEADS), row(mlstm_norm_w[l])],
            "norm_ffn_w": norm_ffn_w[l],
            "ffn": [ffn_conv_w[l], row(ffn_conv_b[l])],
        }
        sample_mix = [state_rwkv_shift[l][:, None, :], state_rwkv_wkv[l],
                      _pad_tail(cache_gdn_conv[l], 3 * GD_W), state_gdn[l],
                      state_mlstm_c[l], state_mlstm_n[l], _pad_cols(state_mlstm_m[l], LANE)[:, None, :]]
        states = ([zero_mix, None], [None, None], [sample_mix, _pad_tail(cache_ffn_conv[l], D_FF)])
        x, xw, ssq, st = _layer(x, xw, ssq, states, lp, wts, l, segs)
        p_out.append(st[1])
        s_out.append(st[2])

    y_prompt = _rmsnorm(x, final_norm_w, F32, seq, ROW_PROMPT, CHUNK).reshape(1, seq, D_MODEL)
    y_sample = _rmsnorm(x, final_norm_w, F32, dec_b * dec_seq, row_sample, CHUNK).reshape(dec_b, dec_seq, D_MODEL)

    def collect(sts):
        stack = lambda f: jnp.stack([f(st) for st in sts], 0)
        return (stack(lambda st: st[0][1]), stack(lambda st: st[0][0][:, 0, :]),
                stack(lambda st: st[0][3]), stack(lambda st: st[0][2][:, TAIL - 3:, :]),
                stack(lambda st: st[0][4]), stack(lambda st: st[0][5]), stack(lambda st: st[0][6][:, 0, :ML_HEADS]),
                stack(lambda st: st[1][:, TAIL - 2:, :]))

    return (y_prompt, y_sample) + collect(p_out) + collect(s_out)
```

```python
import functools
import math

import jax
import jax.numpy as jnp
from jax import lax
from jax.experimental import pallas as pl
from jax.experimental.pallas import tpu as pltpu

F32 = jnp.float32
BF16 = jnp.bfloat16

D_MODEL = 4096
N_META = 16
CHUNK = 64
NORM_EPS = 1e-6
RW_HEADS, RW_HD = 24, 64
RW_W = RW_HEADS * RW_HD
RW_P = 3 * RW_W + 64 + 64 + 128
RW_GN_EPS = 64e-5
GD_HEADS, GD_HD = 12, 128
GD_W = GD_HEADS * GD_HD
GD_P = 4 * GD_W + 2 * GD_HEADS
ML_HEADS, ML_DK, ML_DV = 4, 128, 256
ML_W = ML_HEADS * ML_DV
ML_QK = ML_HEADS * ML_DK
ML_P = 2 * ML_QK + 2 * ML_W + 2 * ML_HEADS
D_FF = 11008

LANE = 128
TAIL = 8
ROW_META = 48
ROW_PROMPT = 64
ROW_TILE = 2128
ROW_TILE_DOWN = 1216
ROW_TILE_SMALL = 448
COL_TILE = 256
FFN_TAIL_ROWS = 512
VMEM_LIMIT = 56 * 1024 * 1024

P_MAIN = RW_P + 3 * GD_W
T_AB = 0
T_IF = 128
T_MLQ = 256
T_MLK = 768
T_Z = 1280
T_MLV = 2816
T_MLO = 3840
P_TAIL = 4864


def _cparams(n_axes):
    return pltpu.CompilerParams(dimension_semantics=("arbitrary",) * n_axes, vmem_limit_bytes=VMEM_LIMIT)


def _bdot(a, b):
    return jnp.dot(a.astype(BF16), b.astype(BF16), preferred_element_type=F32)


def _bdot_nt(a, b):
    return lax.dot_general(a.astype(BF16), b.astype(BF16), (((1,), (1,)), ((), ())), preferred_element_type=F32)


def _bdot_tn(a, b):
    return lax.dot_general(a.astype(BF16), b.astype(BF16), (((0,), (0,)), ((), ())), preferred_element_type=F32)


def _split3(x):
    hi = x.astype(BF16)
    r1 = x - hi.astype(F32)
    mid = r1.astype(BF16)
    lo = (r1 - mid.astype(F32)).astype(BF16)
    return hi, mid, lo


def _cumsum_rows(x, tril_bf):
    hi, mid, lo = _split3(x)
    w = x.shape[1]
    out = jnp.dot(tril_bf, jnp.concatenate([hi, mid, lo], axis=1), preferred_element_type=F32)
    return out[:, :w] + out[:, w:2 * w] + out[:, 2 * w:]


def _cols_to_rows(x, n_rows):
    sel = (lax.broadcasted_iota(jnp.int32, (n_rows, LANE), 0) == lax.broadcasted_iota(jnp.int32, (n_rows, LANE), 1)).astype(BF16)
    hi, mid, lo = _split3(x)
    d = lambda t: lax.dot_general(sel, t, (((1,), (1,)), ((), ())), preferred_element_type=F32)
    return d(hi) + d(mid) + d(lo)


def _head_selectors(width, hd):
    sh = int(math.log2(hd))
    e = (lax.broadcasted_iota(jnp.int32, (width, LANE), 0) >> sh) == lax.broadcasted_iota(jnp.int32, (width, LANE), 1)
    et = lax.broadcasted_iota(jnp.int32, (LANE, width), 0) == (lax.broadcasted_iota(jnp.int32, (LANE, width), 1) >> sh)
    return e.astype(BF16), et.astype(BF16)


def _head_sums(x, sel):
    e, et = sel
    rows = x.shape[0]
    hi, lo, _ = _split3(x)
    part = jnp.dot(jnp.concatenate([hi, lo], axis=0), e, preferred_element_type=F32)
    hi, lo, _ = _split3(part[:rows] + part[rows:])
    back = jnp.dot(jnp.concatenate([hi, lo], axis=0), et, preferred_element_type=F32)
    return back[:rows] + back[rows:]


def _tri_masks(C):
    ri = lax.broadcasted_iota(jnp.int32, (C, C), 0)
    ci = lax.broadcasted_iota(jnp.int32, (C, C), 1)
    return ri >= ci, ri > ci, ri == ci


def _unit_lower_inverse(As, eye, C):
    Ps = [-A for A in As]
    Ts = [jnp.where(eye, 1.0, P) for P in Ps]
    levels = int(math.log2(C))
    if levels > 1:
        Ps = [_bdot(P, P) for P in Ps]
        yield
    for k in range(1, levels):
        if k < levels - 1:
            Ys = [_bdot(jnp.concatenate([T, P], axis=0), P) for T, P in zip(Ts, Ps)]
            Ts = [T + Y[:C] for T, Y in zip(Ts, Ys)]
            Ps = [Y[C:] for Y in Ys]
        else:
            Ts = [T + _bdot(T, P) for T, P in zip(Ts, Ps)]
        yield
    return Ts


def _interleave(gens):
    results, alive = [None] * len(gens), list(range(len(gens)))
    while alive:
        for idx in list(alive):
            try:
                next(gens[idx])
            except StopIteration as stop:
                results[idx] = stop.value
                alive.remove(idx)
    return results


def _sigmoid(x):
    return jax.nn.sigmoid(x)


def _softplus(x):
    return jnp.maximum(x, 0.0) + jnp.log(1.0 + jnp.exp(-jnp.abs(x)))


def _shift_rows(x, tail, s, C):
    xr = pltpu.roll(x, s, 0)
    r8 = lax.broadcasted_iota(jnp.int32, (TAIL, 1), 0)
    top = jnp.where(r8 < s, pltpu.roll(tail, s, 0), xr[0:TAIL])
    return jnp.concatenate([top, xr[TAIL:]], axis=0)


def _rmsnorm_body(x_ref, w_ref, o_ref):
    x = x_ref[...]
    y = x * lax.rsqrt(jnp.mean(x * x, -1, keepdims=True) + NORM_EPS)
    o_ref[...] = (y * w_ref[...]).astype(o_ref.dtype)


def _rmsnorm(x, w, out_dtype, rows, row_base, block):
    d = x.shape[1]
    rb = row_base // block
    return pl.pallas_call(
        _rmsnorm_body,
        grid=(rows // block,),
        in_specs=[pl.BlockSpec((block, d), lambda i: (rb + i, 0)), pl.BlockSpec((1, d), lambda i: (0, 0))],
        out_specs=pl.BlockSpec((block, d), lambda i: (i, 0)),
        out_shape=jax.ShapeDtypeStruct((rows, d), out_dtype),
        compiler_params=_cparams(1),
        name="rmsnorm",
    )(x, w.reshape(1, d))


def _lane_fold(x):
    return functools.reduce(lambda a, b: a + b, [x[:, c:c + LANE] for c in range(0, x.shape[1], LANE)])


def _row_scale(ssq_ref):
    return lax.rsqrt(jnp.sum(ssq_ref[...], -1, keepdims=True) * (1.0 / D_MODEL) + NORM_EPS)


def _prenorm_body(x_ref, w_ref, xw_ref, ssq_ref):
    x = x_ref[...]
    xw_ref[...] = (x * w_ref[...]).astype(xw_ref.dtype)
    ssq_ref[...] = _lane_fold(x * x)


def _prenorm(x, w, block):
    n, d = x.shape
    return pl.pallas_call(
        _prenorm_body, grid=(n // block,),
        in_specs=[pl.BlockSpec((block, d), lambda i: (i, 0)), pl.BlockSpec((1, d), lambda i: (0, 0))],
        out_specs=[pl.BlockSpec((block, d), lambda i: (i, 0)), pl.BlockSpec((block, LANE), lambda i: (i, 0))],
        out_shape=[jax.ShapeDtypeStruct((n, d), BF16), jax.ShapeDtypeStruct((n, LANE), F32)],
        compiler_params=_cparams(1), name="prenorm",
    )(x, w.reshape(1, d))


def _mm_scaled_body(a_ref, ssq_ref, w_ref, o_ref):
    o_ref[...] = _row_scale(ssq_ref) * jnp.dot(a_ref[...], w_ref[...].astype(BF16), preferred_element_type=F32)


def _mm_res_norm_body(a_ref, w_ref, r_ref, nw_ref, o_ref, xw_ref, ssq_ref):
    j = pl.program_id(1)
    acc = r_ref[...] + jnp.dot(a_ref[...], w_ref[...].astype(BF16), preferred_element_type=F32)
    o_ref[...] = acc
    xw_ref[...] = (acc * nw_ref[...]).astype(xw_ref.dtype)
    part = _lane_fold(acc * acc)

    @pl.when(j == 0)
    def _():
        ssq_ref[...] = part

    @pl.when(j > 0)
    def _():
        ssq_ref[...] += part


def _mm_res_body(a_ref, w_ref, r_ref, o_ref):
    o_ref[...] = r_ref[...] + jnp.dot(a_ref[...], w_ref[...].astype(BF16), preferred_element_type=F32)


def _matmul_wide(a, ssq, w, layer, name="matmul_wide"):
    m, k = a.shape
    n = w.shape[2]
    tm, tn = ROW_TILE_DOWN // 2, 4 * COL_TILE
    return pl.pallas_call(
        _mm_scaled_body, grid=(n // tn, m // tm),
        in_specs=[pl.BlockSpec((tm, k), lambda j, i: (i, 0)), pl.BlockSpec((tm, LANE), lambda j, i: (i, 0)),
                  pl.BlockSpec((None, k, tn), lambda j, i: (layer, 0, j))],
        out_specs=pl.BlockSpec((tm, tn), lambda j, i: (i, j)),
        out_shape=jax.ShapeDtypeStruct((m, n), F32),
        compiler_params=_cparams(2), name=name,
    )(a, ssq, w)


def _matmul(a, w, layer, tm, res, norm_w=None, name="matmul"):
    m, k = a.shape
    n = w.shape[2]
    a_spec = pl.BlockSpec((tm, k), lambda i, j: (i, 0), pipeline_mode=pl.Buffered(1))
    w_spec = pl.BlockSpec((None, k, COL_TILE), lambda i, j: (layer, 0, j))
    o_spec = pl.BlockSpec((tm, COL_TILE), lambda i, j: (i, j))
    if norm_w is None:
        return pl.pallas_call(
            _mm_res_body, grid=(m // tm, n // COL_TILE), in_specs=[a_spec, w_spec, o_spec], out_specs=o_spec,
            out_shape=jax.ShapeDtypeStruct((m, n), F32),
            compiler_params=_cparams(2), name=name,
        )(a, w, res)
    return pl.pallas_call(
        _mm_res_norm_body, grid=(m // tm, n // COL_TILE),
        in_specs=[a_spec, w_spec, o_spec, pl.BlockSpec((1, COL_TILE), lambda i, j: (0, j))],
        out_specs=[o_spec, o_spec, pl.BlockSpec((tm, LANE), lambda i, j: (i, 0))],
        out_shape=[jax.ShapeDtypeStruct((m, n), F32), jax.ShapeDtypeStruct((m, n), BF16),
                   jax.ShapeDtypeStruct((m, LANE), F32)],
        compiler_params=_cparams(2), name=name,
    )(a, w, res, norm_w.reshape(1, n))


def _row_spec(width, col_start, seg):
    row_base, C, B, nblk = seg
    rb, cb = row_base // C, col_start // width
    return pl.BlockSpec((C, width), lambda b, i: (rb + b * nblk + i, cb))


def _state_spec(shape):
    nd = len(shape)
    return pl.BlockSpec((1,) + tuple(shape[1:]), lambda b, i: (b,) + (0,) * (nd - 1))


def _param_spec(shape):
    nd = len(shape)
    return pl.BlockSpec(tuple(shape), lambda b, i: (0,) * nd)


def _seq_call(body, seg, row_ins, state_ins, params, buf, out_width, out_col, scratch, name, in_row_base=None):
    row_base, C, B, nblk = seg
    in_seg = seg if in_row_base is None else (in_row_base, C, B, nblk)
    in_specs = [_row_spec(w, c, in_seg) for (_, w, c) in row_ins]
    in_specs += [_state_spec(s.shape) for s in state_ins]
    in_specs += [_param_spec(p.shape) for p in params]
    in_specs += [pl.BlockSpec(memory_space=pl.ANY)]
    args = [a for (a, _, _) in row_ins] + list(state_ins) + list(params) + [buf]
    out_specs = [_row_spec(out_width, out_col, seg)] + [_state_spec(s.shape) for s in state_ins]
    out_shape = [jax.ShapeDtypeStruct(buf.shape, buf.dtype)] + [jax.ShapeDtypeStruct(s.shape, s.dtype) for s in state_ins]
    return pl.pallas_call(
        functools.partial(body, C, nblk), grid=(B, nblk), in_specs=in_specs, out_specs=out_specs,
        out_shape=out_shape, scratch_shapes=scratch,
        input_output_aliases={len(args) - 1: 0},
        compiler_params=_cparams(2), name=name,
    )(*args)


def _rwkv_main(C, p, carry, s_scr, prm):
    mu, w0, wup, a0, aup, gup, kkw, kaw, rkw, lnw, lnb = prm
    rows = lax.broadcasted_iota(jnp.int32, (C, 1), 0)
    prev = jnp.where(rows == 0, carry[...], pltpu.roll(p, 1, 0))
    carry[...] = p[C - 1:C, :]
    xs = p + (prev - p) * mu[...]
    r, k, v = xs[:, 0:RW_W], xs[:, RW_W:2 * RW_W], xs[:, 2 * RW_W:3 * RW_W]
    dw, da, dg = xs[:, 3 * RW_W:3 * RW_W + 64], xs[:, 3 * RW_W + 64:3 * RW_W + 128], xs[:, 3 * RW_W + 128:RW_P]
    lw = -math.exp(-0.5) * _sigmoid(w0[...] + _bdot(jnp.tanh(dw), wup[...]))
    a = _sigmoid(a0[...] + _bdot(da, aup[...]))
    g = _bdot(_sigmoid(dg), gup[...])
    sel = _head_selectors(RW_W, RW_HD)
    kkx = k * kkw[...]
    kkn = kkx * lax.rsqrt(_head_sums(kkx * kkx, sel) + 1e-6)
    kt = k * (1.0 + (a - 1.0) * kaw[...])
    bonus = _head_sums(r * kt * rkw[...], sel) * v
    kb = kkn * a
    yield

    tril, strict, eye = _tri_masks(C)
    cl = _cumsum_rows(lw, tril.astype(BF16))
    w_in = jnp.exp(cl)
    w_ex = jnp.exp(cl - lw)
    w_inv = jnp.exp(-cl)
    w_end = jnp.exp(cl[C - 1:C, :] - cl)

    heads = range(RW_HEADS)
    sls = [slice(RW_HD * h, RW_HD * (h + 1)) for h in heads]
    kk_ex, r_in = kkn * w_ex, r * w_in
    kt_inv, kb_inv = kt * w_inv, kb * w_inv
    kt_end, kb_end = kt * w_end, kb * w_end
    yield
    lhss = [jnp.concatenate([kk_ex[:, sl], r_in[:, sl]], axis=0) for sl in sls]
    rhss = [jnp.concatenate([kt_inv[:, sl], kb_inv[:, sl]], axis=0) for sl in sls]
    scs = [_bdot_nt(lhs, rhs) for lhs, rhs in zip(lhss, rhss)]
    yield
    s0s = [s_scr[h] for h in heads]
    pss = [_bdot_nt(lhs, s0) for lhs, s0 in zip(lhss, s0s)]
    yield
    negs = [-(ps[:C] + _bdot(jnp.where(strict, sc[:C, :C], 0.0), v[:, sl])) for ps, sc, sl in zip(pss, scs, sls)]
    yield
    invs = yield from _unit_lower_inverse([jnp.where(strict, sc[:C, C:], 0.0) for sc in scs], eye, C)
    us = [_bdot(t, n) for t, n in zip(invs, negs)]
    yield
    vus = [jnp.concatenate([v[:, sl], u], axis=0) for u, sl in zip(us, sls)]
    a_rs = [jnp.concatenate([jnp.where(tril, sc[C:, :C], 0.0), jnp.where(tril, sc[C:, C:], 0.0)], axis=1) for sc in scs]
    ys = [ps[C:] + _bdot(a_r, vu) for ps, a_r, vu in zip(pss, a_rs, vus)]
    yield
    kes = [jnp.concatenate([kt_end[:, sl], kb_end[:, sl]], axis=0) for sl in sls]
    for h in heads:
        s_scr[h] = s0s[h] * w_in[C - 1:C, sls[h]] + _bdot_tn(vus[h], kes[h])
    yield
    y = jnp.concatenate(ys, axis=1)
    yc = y - _head_sums(y, sel) * (1.0 / RW_HD)
    yn = yc * lax.rsqrt(_head_sums(yc * yc, sel) * (1.0 / RW_HD) + RW_GN_EPS)
    return (yn * lnw[...] + lnb[...] + bonus) * g


def _gdn_main(C, x, z, ab, tail, s_scr, prm):
    convw, alog, dtb, normw = prm
    t8 = tail[...]
    conv = x * convw[3:4, :]
    for s in (1, 2, 3):
        conv = conv + _shift_rows(x, t8, s, C) * convw[3 - s:4 - s, :]
    tail[...] = x[C - TAIL:, :]
    act = conv * _sigmoid(conv)
    q, k, v = act[:, 0:GD_W], act[:, GD_W:2 * GD_W], act[:, 2 * GD_W:3 * GD_W]
    yield

    g_all = -jnp.exp(alog[...]) * _softplus(ab + dtb[...])
    beta_all = _sigmoid(ab)
    tril, strict, eye = _tri_masks(C)
    gc = _cumsum_rows(g_all, tril.astype(BF16))
    gr = _cols_to_rows(gc, 16)
    eg = jnp.exp(gc)
    yield

    heads = range(GD_HEADS)
    sls = [slice(GD_HD * h, GD_HD * (h + 1)) for h in heads]
    qs = [q[:, sl] for sl in sls]
    qs = [t * (lax.rsqrt(jnp.sum(t * t, -1, keepdims=True) + 1e-6) * GD_HD ** -0.5) for t in qs]
    ks = [k[:, sl] for sl in sls]
    ks = [t * lax.rsqrt(jnp.sum(t * t, -1, keepdims=True) + 1e-6) for t in ks]
    yield
    betas = [beta_all[:, GD_HEADS + h:GD_HEADS + h + 1] for h in heads]
    decays = [jnp.where(tril, jnp.exp(jnp.where(tril, gc[:, h:h + 1] - gr[h:h + 1, :], 0.0)), 0.0) for h in heads]
    yield
    scs = [_bdot_nt(jnp.concatenate([k_h, q_h], axis=0), k_h) for k_h, q_h in zip(ks, qs)]
    yield
    invs = yield from _unit_lower_inverse(
        [jnp.where(strict, beta * sc[:C] * decay, 0.0) for beta, sc, decay in zip(betas, scs, decays)], eye, C)
    rhss = [jnp.concatenate([v[:, sl] * beta, k_h * (beta * eg[:, h:h + 1])], axis=1)
            for h, sl, beta, k_h in zip(heads, sls, betas, ks)]
    sols = [_bdot(t, rhs) for t, rhs in zip(invs, rhss)]
    yield
    s0s = [s_scr[h] for h in heads]
    pss = [_bdot_nt(jnp.concatenate([sol[:, GD_HD:], q_h], axis=0), s0) for sol, q_h, s0 in zip(sols, qs, s0s)]
    yield
    us = [sol[:, :GD_HD] - ps[:C] for sol, ps in zip(sols, pss)]
    os_ = [eg[:, h:h + 1] * ps[C:] + _bdot(sc[C:] * decay, u) for h, ps, sc, decay, u in zip(heads, pss, scs, decays, us)]
    yield
    for h in heads:
        gl = gc[C - 1:C, h:h + 1]
        s_scr[h] = jnp.exp(gl) * s0s[h] + _bdot_tn(us[h] * jnp.exp(gl - gc[:, h:h + 1]), ks[h])
    yield
    outs = [o * lax.rsqrt(jnp.mean(o * o, -1, keepdims=True) + NORM_EPS) * normw[...] for o in os_]
    return jnp.concatenate(outs, axis=1) * (z * _sigmoid(z))


def _mlstm_main(C, q, k, v, og, gates, c_scr, n_scr, m_scr, prm):
    ib, fb, normw = prm
    li_all = gates + ib[...]
    x = gates + fb[...]
    lf_all = jnp.minimum(x, 0.0) - jnp.log(1.0 + jnp.exp(-jnp.abs(x)))
    tril, _, _ = _tri_masks(C)
    fc = _cumsum_rows(lf_all, tril.astype(BF16))
    fr = _cols_to_rows(fc, 8)
    lir = _cols_to_rows(li_all, 8)
    lane = lax.broadcasted_iota(jnp.int32, (1, LANE), 1)
    m_all = m_scr[...]
    m_new = m_all
    yield

    heads = range(ML_HEADS)
    qs = [q[:, ML_DK * h:ML_DK * (h + 1)] * ML_DK ** -0.5 for h in heads]
    ks = [k[:, ML_DK * h:ML_DK * (h + 1)] for h in heads]
    vs = [v[:, ML_DV * h:ML_DV * (h + 1)] for h in heads]
    c0s = [c_scr[h] for h in heads]
    n0s = [n_scr[h:h + 1, :] for h in heads]
    qks = [_bdot_nt(q_h, k_h) for q_h, k_h in zip(qs, ks)]
    qcs = [_bdot_nt(q_h, c0) for q_h, c0 in zip(qs, c0s)]
    yield
    fcs = [fc[:, ML_HEADS + h:ML_HEADS + h + 1] for h in heads]
    m0s = [m_all[:, h:h + 1] for h in heads]
    dmats = [jnp.where(tril, fc_h - fr[ML_HEADS + h:ML_HEADS + h + 1, :] + lir[h:h + 1, :], -jnp.inf)
             for h, fc_h in zip(heads, fcs)]
    inters = [fc_h + m0 for fc_h, m0 in zip(fcs, m0s)]
    ms = [jnp.maximum(inter, jnp.max(dmat, -1, keepdims=True)) for inter, dmat in zip(inters, dmats)]
    yield
    ss = [qk * jnp.exp(dmat - m) for qk, dmat, m in zip(qks, dmats, ms)]
    wis = [jnp.exp(inter - m) for inter, m in zip(inters, ms)]
    yield
    nums = [wi * qc + _bdot(s, v_h) for wi, qc, s, v_h in zip(wis, qcs, ss, vs)]
    dens = [wi * jnp.sum(q_h * n0, -1, keepdims=True) + jnp.sum(s, -1, keepdims=True)
            for wi, q_h, n0, s in zip(wis, qs, n0s, ss)]
    yield
    hhs = [num / jnp.maximum(jnp.abs(den), jnp.exp(-m)) for num, den, m in zip(nums, dens, ms)]
    m_cs = [m[C - 1:C, :] for m in ms]
    wends = [jnp.exp(fc_h[C - 1:C, :] - fc_h + li_all[:, h:h + 1] - m_c) for h, fc_h, m_c in zip(heads, fcs, m_cs)]
    dstates = [jnp.exp(fc_h[C - 1:C, :] + m0 - m_c) for fc_h, m0, m_c in zip(fcs, m0s, m_cs)]
    for h in heads:
        c_scr[h] = dstates[h] * c0s[h] + _bdot_tn(vs[h] * wends[h], ks[h])
        n_scr[h:h + 1, :] = dstates[h] * n0s[h] + jnp.sum(wends[h] * ks[h], 0, keepdims=True)
        m_new = jnp.where(lane == h, m_cs[h], m_new)
    m_scr[...] = m_new
    yield
    outs = [hh * lax.rsqrt(jnp.mean(hh * hh, -1, keepdims=True) + NORM_EPS) for hh in hhs]
    return jnp.concatenate(outs, axis=1) * normw[...] * _sigmoid(og)


N_MIX_STATES = 7
N_RW_PRM, N_GD_PRM, N_ML_PRM = 11, 4, 3


def _mixer_body(C, nblk, p_ref, *refs):
    ins, refs = refs[:N_MIX_STATES], refs[N_MIX_STATES:]
    rw_prm, refs = refs[:N_RW_PRM], refs[N_RW_PRM:]
    gd_prm, refs = refs[:N_GD_PRM], refs[N_GD_PRM:]
    ml_prm, refs = refs[:N_ML_PRM], refs[N_ML_PRM:]
    o_ref, refs = refs[1], refs[2:]
    outs, scr = refs[:N_MIX_STATES], refs[N_MIX_STATES:]
    carry, s_rw, tail, s_gd, c_scr, n_scr, m_scr = scr
    i = pl.program_id(1)

    @pl.when(i == 0)
    def _():
        for dst, src in zip(scr, ins):
            dst[...] = src[0]

    pm, pt = p_ref[:, 0:P_MAIN], p_ref[:, P_MAIN:P_MAIN + P_TAIL]
    o_rw, o_gd, o_ml = _interleave([
        _rwkv_main(C, pm[:, 0:RW_P], carry, s_rw, rw_prm),
        _gdn_main(C, pm[:, RW_P:P_MAIN], pt[:, T_Z:T_Z + GD_W], pt[:, T_AB:T_AB + LANE], tail, s_gd, gd_prm),
        _mlstm_main(C, pt[:, T_MLQ:T_MLQ + ML_QK], pt[:, T_MLK:T_MLK + ML_QK], pt[:, T_MLV:T_MLV + ML_W],
                    pt[:, T_MLO:T_MLO + ML_W], pt[:, T_IF:T_IF + LANE], c_scr, n_scr, m_scr, ml_prm)])
    o_ref[:, 0:RW_W] = o_rw.astype(o_ref.dtype)
    o_ref[:, RW_W:RW_W + GD_W] = o_gd.astype(o_ref.dtype)
    o_ref[:, RW_W + GD_W:] = o_ml.astype(o_ref.dtype)

    @pl.when(i == nblk - 1)
    def _():
        outs[0][0] = pm[C - 1:C, 0:RW_P]
        for dst, src in zip(outs[1:], scr[1:]):
            dst[0] = src[...]


def _mixer_call(p, buf, seg, states, params):
    scratch = [pltpu.VMEM((1, RW_P), F32), pltpu.VMEM((RW_HEADS, RW_HD, RW_HD), F32),
               pltpu.VMEM((TAIL, 3 * GD_W), F32), pltpu.VMEM((GD_HEADS, GD_HD, GD_HD), F32),
               pltpu.VMEM((ML_HEADS, ML_DV, ML_DK), F32), pltpu.VMEM((ML_HEADS, ML_DK), F32), pltpu.VMEM((1, LANE), F32)]
    outs = _seq_call(_mixer_body, seg, [(p, P_MAIN + P_TAIL, 0)], list(states), params,
                     buf, D_MODEL, 0, scratch, "mixers")
    return outs[0], list(outs[1:])


def _ffn_act_body(C, nblk, g_ref, u_ref, tail_in, convw, convb, _buf, o_ref, tail_out, tail):
    i = pl.program_id(1)

    @pl.when(i == 0)
    def _():
        tail[...] = tail_in[0]

    x = g_ref[...]
    t8 = tail[...]
    conv = x * convw[2:3, :] + convb[...]
    for s in (1, 2):
        conv = conv + _shift_rows(x, t8, s, C) * convw[2 - s:3 - s, :]
    tail[...] = x[C - TAIL:, :]
    o_ref[...] = (conv * _sigmoid(conv) * u_ref[...]).astype(o_ref.dtype)

    @pl.when(i == nblk - 1)
    def _():
        tail_out[0] = tail[...]


def _ffn_act_call(g, u, buf, seg, tail, params, in_row_base=None):
    outs = _seq_call(
        _ffn_act_body, seg, [(g, D_FF, 0), (u, D_FF, 0)], [tail], params, buf, D_FF, 0,
        [pltpu.VMEM((TAIL, D_FF), F32)], "ffn_act", in_row_base)
    return outs[0], outs[1]


def _ffn_up_body(tm, h_ref, ssq_ref, wg_ref, wu_ref, convw, convb, o_ref, gt_ref, ut_ref, carry):
    i, j = pl.program_id(0), pl.program_id(1)
    h = h_ref[...]
    scale = _row_scale(ssq_ref)
    g = scale * jnp.dot(h, wg_ref[...].astype(BF16), preferred_element_type=F32)
    u = scale * jnp.dot(h, wu_ref[...].astype(BF16), preferred_element_type=F32)
    gt_ref[...] = g[tm - FFN_TAIL_ROWS:, :]
    ut_ref[...] = u[tm - FFN_TAIL_ROWS:, :]
    rows = i * tm + lax.broadcasted_iota(jnp.int32, (tm, 1), 0)
    g = jnp.where(rows >= ROW_META, g, 0.0)

    @pl.when(i == 0)
    def _():
        carry[j] = jnp.zeros((TAIL, COL_TILE), F32)

    t8 = carry[j]
    conv = g * convw[2:3, :] + convb[...]
    for s in (1, 2):
        conv = conv + _shift_rows(g, t8, s, tm) * convw[2 - s:3 - s, :]
    carry[j] = g[tm - TAIL:, :]
    o_ref[...] = (conv * _sigmoid(conv) * u).astype(o_ref.dtype)


def _ffn_up(hb, ssq, w_gate, w_up, layer, convw, convb, tm):
    m, k = hb.shape
    n = w_gate.shape[2]
    w_spec = pl.BlockSpec((None, k, COL_TILE), lambda i, j: (layer, 0, j))
    t_spec = pl.BlockSpec((FFN_TAIL_ROWS, COL_TILE), lambda i, j: (i, j))
    return pl.pallas_call(
        functools.partial(_ffn_up_body, tm), grid=(m // tm, n // COL_TILE),
        in_specs=[pl.BlockSpec((tm, k), lambda i, j: (i, 0), pipeline_mode=pl.Buffered(1)),
                  pl.BlockSpec((tm, LANE), lambda i, j: (i, 0)), w_spec, w_spec,
                  pl.BlockSpec((3, COL_TILE), lambda i, j: (0, j)), pl.BlockSpec((1, COL_TILE), lambda i, j: (0, j))],
        out_specs=[pl.BlockSpec((tm, COL_TILE), lambda i, j: (i, j)), t_spec, t_spec],
        out_shape=[jax.ShapeDtypeStruct((m, n), BF16), jax.ShapeDtypeStruct((m // tm * FFN_TAIL_ROWS, n), F32),
                   jax.ShapeDtypeStruct((m // tm * FFN_TAIL_ROWS, n), F32)],
        scratch_shapes=[pltpu.VMEM((n // COL_TILE, TAIL, COL_TILE), F32)],
        compiler_params=_cparams(2), name="ffn_up_act",
    )(hb, ssq, w_gate, w_up, convw, convb)


def _pad_cols(x, width):
    return jnp.pad(x, ((0, 0),) * (x.ndim - 1) + ((0, width - x.shape[-1]),))


def _lane_vec(x, start=0):
    return jnp.zeros((1, LANE), F32).at[0, start:start + x.shape[0]].set(x)


def _tail_cols(w):
    ml0 = RW_P + GD_P
    gd_ab = w[..., P_MAIN:P_MAIN + 2 * GD_HEADS]
    ml_if = w[..., ml0 + 2 * ML_QK + ML_W:ml0 + 2 * ML_QK + ML_W + 2 * ML_HEADS]
    cols = [_pad_cols(gd_ab, LANE), _pad_cols(ml_if, LANE), w[..., ml0:ml0 + 2 * ML_QK],
            w[..., P_MAIN + 2 * GD_HEADS:ml0], w[..., ml0 + 2 * ML_QK:ml0 + 2 * ML_QK + ML_W],
            w[..., ml0 + 2 * ML_QK + ML_W + 2 * ML_HEADS:ml0 + ML_P]]
    return jnp.concatenate(cols, axis=-1)


def _pad_tail(buf, width):
    b, r, w = buf.shape
    return jnp.pad(buf, ((0, 0), (TAIL - r, 0), (0, width - w)))


def _layer(x, xw, ssq, states, lp, wts, layer, segs):
    n = x.shape[0]
    p = _matmul_wide(xw, ssq, wts["w_in"], layer, name="proj_in")
    mix = jnp.zeros((n, D_MODEL), BF16)
    new_states = []
    for seg, (mix_st, ffn_tail) in zip(segs, states):
        if mix_st is None:
            mix_st = new_states[-1][0]
        mix, mix_new = _mixer_call(p, mix, seg, mix_st, lp["mixers"])
        new_states.append([mix_new, ffn_tail])
    x, xw, ssq = _matmul(mix, wts["w_out"], layer, ROW_TILE, x, norm_w=lp["norm_ffn_w"], name="proj_out")
    act, g_tail, u_tail = _ffn_up(xw, ssq, wts["w_gate"], wts["w_up"], layer, lp["ffn"][0], lp["ffn"][1], ROW_TILE)
    tail_base = n - g_tail.shape[0]
    for idx, seg in enumerate(segs):
        if idx == 0 or states[idx][0] is None:
            end = seg[0] + seg[1] * seg[2] * seg[3]
            kept = end - TAIL >= n - FFN_TAIL_ROWS
            new_states[idx][1] = g_tail[end - TAIL - tail_base:end - tail_base][None] if kept else None
        else:
            assert seg[0] >= n - FFN_TAIL_ROWS
            act, new_states[idx][1] = _ffn_act_call(g_tail, u_tail, act, seg, new_states[idx][1], lp["ffn"],
                                                    in_row_base=seg[0] - tail_base)
    if lp["next_norm_w"] is None:
        return _matmul(act, wts["w_down"], layer, ROW_TILE_DOWN, x, name="ffn_down"), None, None, new_states
    x, xw, ssq = _matmul(act, wts["w_down"], layer, ROW_TILE_DOWN, x, norm_w=lp["next_norm_w"], name="ffn_down")
    return x, xw, ssq, new_states


def kernel(x_prompt, x_sample, state_rwkv_wkv, state_rwkv_shift, state_gdn, cache_gdn_conv, state_mlstm_c, state_mlstm_n, state_mlstm_m, cache_ffn_conv, meta_tokens, norm_mix_w, w_in, rwkv_mu, rwkv_w0, rwkv_w_up, rwkv_a0, rwkv_a_up, rwkv_g_up, rwkv_k_k, rwkv_k_a, rwkv_r_k, rwkv_ln_w, rwkv_ln_b, gdn_conv_w, gdn_a_log, gdn_dt_bias, gdn_norm_w, mlstm_i_b, mlstm_f_b, mlstm_norm_w, w_out, norm_ffn_w, ffn_w_gate, ffn_w_up, ffn_conv_w, ffn_conv_b, ffn_w_down, final_norm_w):
    depth = w_in.shape[0]
    seq = x_prompt.shape[1]
    dec_b, dec_seq = x_sample.shape[0], x_sample.shape[1]
    row_sample = ROW_PROMPT + seq
    n_rows = row_sample + dec_b * dec_seq
    n_pad = -(-n_rows // ROW_TILE) * ROW_TILE
    segs = ((ROW_META, N_META, 1, 1), (ROW_PROMPT, CHUNK, 1, seq // CHUNK), (row_sample, dec_seq, dec_b, 1))

    x = jnp.concatenate([
        jnp.zeros((ROW_META, D_MODEL), F32), meta_tokens.astype(F32), x_prompt[0],
        x_sample.reshape(dec_b * dec_seq, D_MODEL), jnp.zeros((n_pad - n_rows, D_MODEL), F32)], axis=0)

    row = lambda a: a.reshape(1, -1)
    zero_mix = [jnp.zeros((1, 1, RW_P), F32), jnp.zeros((1, RW_HEADS, RW_HD, RW_HD), F32),
                jnp.zeros((1, TAIL, 3 * GD_W), F32), jnp.zeros((1, GD_HEADS, GD_HD, GD_HD), F32),
                jnp.zeros((1, ML_HEADS, ML_DV, ML_DK), F32), jnp.zeros((1, ML_HEADS, ML_DK), F32),
                jnp.zeros((1, 1, LANE), F32)]

    wts = {
        "w_in": jnp.concatenate([w_in[..., :P_MAIN], _tail_cols(w_in)], axis=-1).astype(BF16),
        "w_out": w_out,
        "w_gate": ffn_w_gate,
        "w_up": ffn_w_up,
        "w_down": ffn_w_down.astype(BF16),
    }

    xw, ssq = _prenorm(x, norm_mix_w[0], ROW_TILE_SMALL)
    p_out, s_out = [], []
    for l in range(depth):
        lp = {
            "next_norm_w": norm_mix_w[l + 1] if l + 1 < depth else None,
            "mixers": [row(rwkv_mu[l]), row(rwkv_w0[l]), rwkv_w_up[l], row(rwkv_a0[l]), rwkv_a_up[l], rwkv_g_up[l],
                       row(rwkv_k_k[l]), row(rwkv_k_a[l]), row(rwkv_r_k[l]), row(rwkv_ln_w[l]), row(rwkv_ln_b[l]),
                       gdn_conv_w[l], _lane_vec(gdn_a_log[l]), _lane_vec(gdn_dt_bias[l]), row(gdn_norm_w[l]),
                       _lane_vec(mlstm_i_b[l]), _lane_vec(mlstm_f_b[l], ML_HEADS), row(mlstm_norm_w[l])],
            "norm_ffn_w": norm_ffn_w[l],
            "ffn": [ffn_conv_w[l], row(ffn_conv_b[l])],
        }
        sample_mix = [state_rwkv_shift[l][:, None, :], state_rwkv_wkv[l],
                      _pad_tail(cache_gdn_conv[l], 3 * GD_W), state_gdn[l],
                      state_mlstm_c[l], state_mlstm_n[l], _pad_cols(state_mlstm_m[l], LANE)[:, None, :]]
        states = ([zero_mix, None], [None, None], [sample_mix, _pad_tail(cache_ffn_conv[l], D_FF)])
        x, xw, ssq, st = _layer(x, xw, ssq, states, lp, wts, l, segs)
        p_out.append(st[1])
        s_out.append(st[2])

    y_prompt = _rmsnorm(x, final_norm_w, F32, seq, ROW_PROMPT, CHUNK).reshape(1, seq, D_MODEL)
    y_sample = _rmsnorm(x, final_norm_w, F32, dec_b * dec_seq, row_sample, CHUNK).reshape(dec_b, dec_seq, D_MODEL)

    def collect(sts):
        stack = lambda f: jnp.stack([f(st) for st in sts], 0)
        return (stack(lambda st: st[0][1]), stack(lambda st: st[0][0][:, 0, :]),
                stack(lambda st: st[0][3]), stack(lambda st: st[0][2][:, TAIL - 3:, :]),
                stack(lambda st: st[0][4]), stack(lambda st: st[0][5]), stack(lambda st: st[0][6][:, 0, :ML_HEADS]),
                stack(lambda st: st[1][:, TAIL - 2:, :]))

    return (y_prompt, y_sample) + collect(p_out) + collect(s_out)
```
